```python
import math
import jax
import jax.numpy as jnp
from jax import lax
import numpy as np

D_MODEL = 1024
BATCH = 4
SEQ = 4096
DEPTH = 4
DEC_BATCH = 128
DEC_SEQ = 1
PAST_LEN = 2048
PAGE_SIZE = 128

N_BRANCH = 4
MIX_W = D_MODEL // 4
LRU_WIDTH = MIX_W
LRU_BLOCKS = 4
LRU_BLOCK_DIM = LRU_WIDTH // LRU_BLOCKS
CONV_WIDTH = 4
LRU_C = 8.0
DIFF_HEADS = 4
DIFF_HEAD_DIM = MIX_W // DIFF_HEADS
DIFF_QK_DIM = DIFF_HEAD_DIM // 2
SB_HEADS = 4
SB_HEAD_DIM = MIX_W // SB_HEADS
HG_HEADS = 4
HG_KDIM = MIX_W // HG_HEADS
HG_VDIM = MIX_W // HG_HEADS
HG_CHUNK = 64
N_GROUPS = 4
EXPERTS_PER_GROUP = 4
N_EXPERTS = N_GROUPS * EXPERTS_PER_GROUP
TOP_K_IN_GROUP = 2
EXPERT_FF = D_MODEL // 4
BLOCK_Q = 128
LN_EPS = 1e-5
RMS_EPS = 1e-5
NEG_BIG = -1e30
F_FLOOR = 1e-30
DEEPNORM_ALPHA = (2 * DEPTH) ** 0.25
DEEPNORM_BETA = (8 * DEPTH) ** -0.25
IN_SPLITS = (LRU_WIDTH, LRU_WIDTH, MIX_W, MIX_W, MIX_W, MIX_W, MIX_W, MIX_W,
             MIX_W, MIX_W, MIX_W, MIX_W, N_BRANCH * D_MODEL)
IN_COLS = sum(IN_SPLITS)

kernel_name = 'hybrid_rglru_diffattn_stickbreak_hgrn2_hmoe_step'


def layer_norm(x, g, b):
    xf = x.astype(jnp.float32)
    mu = jnp.mean(xf, axis=-1, keepdims=True)
    var = jnp.mean(jnp.square(xf - mu), axis=-1, keepdims=True)
    return ((xf - mu) * lax.rsqrt(var + LN_EPS)).astype(x.dtype) * g + b


def rms_norm(x, g):
    xf = x.astype(jnp.float32)
    inv = lax.rsqrt(jnp.mean(jnp.square(xf), axis=-1, keepdims=True) + RMS_EPS)
    return (xf * inv).astype(x.dtype) * g


def alibi_slopes(n_heads):
    return jnp.asarray([2.0 ** (-8.0 * (h + 1) / n_heads) for h in range(n_heads)], dtype=jnp.float32)


def split_cols(u):
    outs, start = [], 0
    for width in IN_SPLITS:
        outs.append(u[..., start:start + width])
        start += width
    return outs


def gather_pages(pool, page_table):
    rows = pool[page_table]
    return rows.reshape((page_table.shape[0], page_table.shape[1] * pool.shape[1]) + pool.shape[2:])


def sweep_query_blocks(attend, q, q_pos):
    bsz, t = q.shape[0], q.shape[1]
    nb = t // BLOCK_Q
    qb = jnp.swapaxes(q.reshape((bsz, nb, BLOCK_Q) + q.shape[2:]), 0, 1)
    pb = q_pos.reshape(nb, BLOCK_Q)
    out = lax.map(lambda args: attend(args[0], args[1]), (qb, pb))
    return jnp.swapaxes(out, 0, 1).reshape((bsz, t) + out.shape[3:])


def _linear_combine(c1, c2):
    a1, b1 = c1
    a2, b2 = c2
    return a1 * a2, a2 * b1 + b2


def rglru_branch(xa, ga, conv_buf, h0, pos, p):
    bsz, t, _ = xa.shape
    xpad = jnp.concatenate([conv_buf.astype(xa.dtype), xa], axis=1)
    xc = p['conv_b'] + sum(p['conv_w'][i] * xpad[:, i:i + t] for i in range(CONV_WIDTH))
    xb = xc.reshape(bsz, t, LRU_BLOCKS, LRU_BLOCK_DIM)
    r = jax.nn.sigmoid(jnp.einsum('bthi,hij->bthj', xb, p['lru_wa']).reshape(bsz, t, LRU_WIDTH) + p['lru_ba'])
    i_g = jax.nn.sigmoid(jnp.einsum('bthi,hij->bthj', xb, p['lru_wx']).reshape(bsz, t, LRU_WIDTH) + p['lru_bx'])
    log_a = -LRU_C * r * jax.nn.softplus(-p['lru_lambda'])
    a = jnp.exp(log_a)
    mult = jnp.where((pos == 0)[None, :, None], 1.0, jnp.sqrt(jnp.maximum(-jnp.expm1(2.0 * log_a), 0.0)))
    b = mult * i_g * xc
    b = b.at[:, 0].add(a[:, 0] * h0.astype(b.dtype))
    _, h = lax.associative_scan(_linear_combine, (a, b), axis=1)
    y = jax.nn.gelu(ga) * h
    return y, xpad[:, t:], h[:, -1]


def diff_attend(q, k, v, q_pos, k_pos, lam, slopes):
    scale = DIFF_QK_DIM ** -0.5
    dist = (q_pos[:, None] - k_pos[None, :]).astype(jnp.float32)
    allowed = dist >= 0
    bias = -slopes[:, None, None] * dist[None]

    def softmax_map(qa, ka):
        s = jnp.einsum('bqhd,bkhd->bhqk', qa, ka).astype(jnp.float32) * scale + bias
        return jax.nn.softmax(jnp.where(allowed, s, NEG_BIG), axis=-1)

    attn = (softmax_map(q[..., :DIFF_QK_DIM], k[..., :DIFF_QK_DIM])
            - lam * softmax_map(q[..., DIFF_QK_DIM:], k[..., DIFF_QK_DIM:]))
    return jnp.einsum('bhqk,bkhd->bqhd', attn.astype(v.dtype), v)


def sb_attend(q, k, v, q_pos, k_pos):
    z = jnp.einsum('bqhd,bkhd->bhqk', q, k).astype(jnp.float32) * (SB_HEAD_DIM ** -0.5)
    earlier = k_pos[None, :] < q_pos[:, None]
    log_keep = jnp.where(earlier, jax.nn.log_sigmoid(-z), 0.0)
    later = lax.cumsum(log_keep, axis=3, reverse=True) - log_keep
    w = jnp.where(earlier, jnp.exp(jnp.minimum(jax.nn.log_sigmoid(z) + later, 0.0)), 0.0)
    return jnp.einsum('bhqk,bkhd->bqhd', w.astype(v.dtype), v)


def hgrn_chunk(s0, q, log_f, k, v):
    c = q.shape[1]
    cum = jnp.cumsum(log_f, axis=1)
    causal = jnp.tril(jnp.ones((c, c), dtype=bool))[None, :, :, None, None]
    decay = jnp.exp(jnp.where(causal, cum[:, :, None] - cum[:, None, :], NEG_BIG))
    scores = jnp.einsum('bthk,btshk,bshk->bhts', q, decay, k)
    o = (jnp.einsum('bhts,bshv->bthv', scores, v)
         + jnp.einsum('bthk,bhkv->bthv', q * jnp.exp(cum), s0))
    last = cum[:, -1]
    s_new = (jnp.exp(last)[..., None] * s0
             + jnp.einsum('bshk,bshv->bhkv', k * jnp.exp(last[:, None] - cum), v))
    return s_new, o


def hgrn_branch(hq, hf, hi, hg, s0, lb, chunk, p):
    bsz, t, _ = hq.shape
    shp = (bsz, t, HG_HEADS, HG_KDIM)
    q = jax.nn.silu(hq).reshape(shp).astype(jnp.float32)
    lb = lb.reshape(HG_HEADS, HG_KDIM)
    zf = hf.reshape(shp).astype(jnp.float32)
    f = lb + (1.0 - lb) * jax.nn.sigmoid(zf)
    log_f = jnp.log(jnp.maximum(f, F_FLOOR))
    k = (1.0 - lb) * jax.nn.sigmoid(-zf)
    v = hi.reshape(bsz, t, HG_HEADS, HG_VDIM).astype(jnp.float32)
    nc = t // chunk

    def to_chunks(a):
        return jnp.swapaxes(a.reshape((bsz, nc, chunk) + a.shape[2:]), 0, 1)

    s_last, o = lax.scan(lambda s, xs: hgrn_chunk(s, *xs), s0.astype(jnp.float32),
                         (to_chunks(q), to_chunks(log_f), to_chunks(k), to_chunks(v)))
    o = jnp.swapaxes(o, 0, 1).reshape(bsz, t, HG_HEADS, HG_VDIM).astype(hq.dtype)
    y = rms_norm(o, p['hgrn_norm_g']) * jax.nn.silu(hg.reshape(bsz, t, HG_HEADS, HG_VDIM))
    return y.reshape(bsz, t, MIX_W), s_last.astype(hq.dtype)


def token_mixers(x, pos, p, layer, hg_lb, past):
    bsz, t, _ = x.shape
    u = jnp.einsum('btd,dc->btc', x, p['w_in'])
    xa, ga, dq, dk, dv, sq, sk, sv, hq, hf, hi, hg, gate_logits = split_cols(u)

    if past is None:
        conv_buf = jnp.zeros((bsz, CONV_WIDTH - 1, LRU_WIDTH), x.dtype)
        h0 = jnp.zeros((bsz, LRU_WIDTH), x.dtype)
    else:
        conv_buf, h0 = past['conv'], past['lru']
    ya, conv_new, h_new = rglru_branch(xa, ga, conv_buf, h0, pos, p)

    dq = dq.reshape(bsz, t, DIFF_HEADS, 2 * DIFF_QK_DIM)
    dk = dk.reshape(bsz, t, DIFF_HEADS, 2 * DIFF_QK_DIM)
    dv = dv.reshape(bsz, t, DIFF_HEADS, DIFF_HEAD_DIM)
    sq = sq.reshape(bsz, t, SB_HEADS, SB_HEAD_DIM)
    sk = sk.reshape(bsz, t, SB_HEADS, SB_HEAD_DIM)
    sv = sv.reshape(bsz, t, SB_HEADS, SB_HEAD_DIM)
    lam_init = 0.8 - 0.6 * math.exp(-0.3 * layer)
    lam = (jnp.exp(jnp.sum(p['diff_lam_q1'] * p['diff_lam_k1']).astype(jnp.float32))
           - jnp.exp(jnp.sum(p['diff_lam_q2'] * p['diff_lam_k2']).astype(jnp.float32)) + lam_init)
    slopes = alibi_slopes(DIFF_HEADS)
    if past is None:
        dk_all, dv_all, sk_all, sv_all, k_pos = dk, dv, sk, sv, pos
    else:
        dk_all = jnp.concatenate([past['diff_k'].astype(dk.dtype), dk], axis=1)
        dv_all = jnp.concatenate([past['diff_v'].astype(dv.dtype), dv], axis=1)
        sk_all = jnp.concatenate([past['sb_k'].astype(sk.dtype), sk], axis=1)
        sv_all = jnp.concatenate([past['sb_v'].astype(sv.dtype), sv], axis=1)
        k_pos = jnp.concatenate([jnp.arange(past['diff_k'].shape[1], dtype=pos.dtype), pos])

    def diff_fn(qb, pb):
        return diff_attend(qb, dk_all, dv_all, pb, k_pos, lam, slopes)

    def sb_fn(qb, pb):
        return sb_attend(qb, sk_all, sv_all, pb, k_pos)

    if past is None:
        yb = sweep_query_blocks(diff_fn, dq, pos)
        yc = sweep_query_blocks(sb_fn, sq, pos)
    else:
        yb = diff_fn(dq, pos)
        yc = sb_fn(sq, pos)
    yb = (rms_norm(yb, p['diff_norm_g']) * (1.0 - lam_init)).reshape(bsz, t, MIX_W)
    yc = yc.reshape(bsz, t, MIX_W)

    if past is None:
        s0 = jnp.zeros((bsz, HG_HEADS, HG_KDIM, HG_VDIM), jnp.float32)
        chunk = HG_CHUNK
    else:
        s0 = past['hgrn']
        chunk = t
    yd, s_new = hgrn_branch(hq, hf, hi, hg, s0, hg_lb, chunk, p)

    branches = jnp.stack([ya, yb, yc, yd], axis=2)
    proj = jnp.einsum('btnc,ncd->btnd', branches, p['w_branch'])
    gates = jax.nn.sigmoid(gate_logits.reshape(bsz, t, N_BRANCH, D_MODEL))
    merged = jnp.sum(gates * proj, axis=2)
    out = jnp.einsum('btd,de->bte', merged, p['w_out'])
    return out, (dk, dv, sk, sv, conv_new, h_new, s_new)


def hier_moe(x, p):
    bsz, t, d = x.shape
    xt = x.reshape(bsz * t, d)
    g_logits = (xt @ p['router_group_w'] + p['router_group_b']).astype(jnp.float32)
    g_idx = jnp.argmax(g_logits, axis=-1)
    g_w = jnp.take_along_axis(jax.nn.softmax(g_logits, axis=-1), g_idx[:, None], axis=1)
    e_logits = (xt @ p['router_expert_w'] + p['router_expert_b']).astype(jnp.float32)
    e_logits = e_logits.reshape(-1, N_GROUPS, EXPERTS_PER_GROUP)
    e_in_group = jnp.take_along_axis(e_logits, g_idx[:, None, None], axis=1)[:, 0]
    top_v, top_i = lax.top_k(e_in_group, TOP_K_IN_GROUP)
    top_w = jax.nn.softmax(top_v, axis=-1) * g_w
    expert_id = g_idx[:, None] * EXPERTS_PER_GROUP + top_i
    combine = jnp.sum(jax.nn.one_hot(expert_id, N_EXPERTS, dtype=jnp.float32) * top_w[..., None], axis=1)
    hid = jax.nn.silu(jnp.einsum('nd,edf->nef', xt, p['exp_w1'])) * jnp.einsum('nd,edf->nef', xt, p['exp_w3'])
    hid = hid * combine[:, :, None].astype(hid.dtype)
    return jnp.einsum('nef,efd->nd', hid, p['exp_w2']).reshape(bsz, t, d)


def trunk_layer(x, pos, p, layer, hg_lb, past):
    mix, new_state = token_mixers(x, pos, p, layer, hg_lb, past)
    x = layer_norm(DEEPNORM_ALPHA * x + mix, p['ln1_g'], p['ln1_b'])
    x = layer_norm(DEEPNORM_ALPHA * x + hier_moe(x, p), p['ln2_g'], p['ln2_b'])
    return x, new_state


def setup_inputs(seed: int = 0) -> dict:
    key = jax.random.key(seed)
    keys = jax.random.split(key, 40)

    def nrm(i, shape, scale):
        return jax.random.normal(keys[i], shape, jnp.float32) * scale

    n_pages = PAST_LEN // PAGE_SIZE
    n_used = DEC_BATCH * n_pages
    n_pool = n_used + n_used // 4
    perm = jax.random.permutation(keys[6], n_pool)
    page_table = perm[:n_used].reshape(DEC_BATCH, n_pages).astype(jnp.int32)
    u_lam = jax.random.uniform(keys[17], (DEPTH, LRU_WIDTH), jnp.float32, minval=0.9, maxval=0.999)
    return {
        'x_prompt': nrm(0, (BATCH, SEQ, D_MODEL), 1.0),
        'x_sample': nrm(1, (DEC_BATCH, DEC_SEQ, D_MODEL), 1.0),
        'cache_diff_k': nrm(2, (DEPTH, n_pool, PAGE_SIZE, DIFF_HEADS, 2 * DIFF_QK_DIM), 1.0),
        'cache_diff_v': nrm(3, (DEPTH, n_pool, PAGE_SIZE, DIFF_HEADS, DIFF_HEAD_DIM), 1.0),
        'cache_sb_k': nrm(4, (DEPTH, n_pool, PAGE_SIZE, SB_HEADS, SB_HEAD_DIM), 1.0),
        'cache_sb_v': nrm(5, (DEPTH, n_pool, PAGE_SIZE, SB_HEADS, SB_HEAD_DIM), 1.0),
        'page_table': page_table,
        'state_conv': nrm(7, (DEPTH, DEC_BATCH, CONV_WIDTH - 1, LRU_WIDTH), 1.0),
        'state_lru': nrm(8, (DEPTH, DEC_BATCH, LRU_WIDTH), 0.5),
        'state_hgrn': nrm(9, (DEPTH, DEC_BATCH, HG_HEADS, HG_KDIM, HG_VDIM), 0.5),
        'w_in': nrm(10, (DEPTH, D_MODEL, IN_COLS), D_MODEL ** -0.5),
        'conv_w': nrm(11, (DEPTH, CONV_WIDTH, LRU_WIDTH), CONV_WIDTH ** -0.5),
        'conv_b': nrm(12, (DEPTH, LRU_WIDTH), 0.01),
        'lru_wa': nrm(13, (DEPTH, LRU_BLOCKS, LRU_BLOCK_DIM, LRU_BLOCK_DIM), LRU_BLOCK_DIM ** -0.5),
        'lru_ba': nrm(14, (DEPTH, LRU_WIDTH), 0.01),
        'lru_wx': nrm(15, (DEPTH, LRU_BLOCKS, LRU_BLOCK_DIM, LRU_BLOCK_DIM), LRU_BLOCK_DIM ** -0.5),
        'lru_bx': nrm(16, (DEPTH, LRU_WIDTH), 0.01),
        'lru_lambda': jnp.log(u_lam) - jnp.log1p(-u_lam),
        'diff_lam_q1': nrm(18, (DEPTH, DIFF_QK_DIM), 0.1),
        'diff_lam_k1': nrm(19, (DEPTH, DIFF_QK_DIM), 0.1),
        'diff_lam_q2': nrm(20, (DEPTH, DIFF_QK_DIM), 0.1),
        'diff_lam_k2': nrm(21, (DEPTH, DIFF_QK_DIM), 0.1),
        'diff_norm_g': 1.0 + nrm(22, (DEPTH, DIFF_HEAD_DIM), 0.02),
        'hgrn_lb_raw': nrm(23, (DEPTH, MIX_W), 0.1),
        'hgrn_norm_g': 1.0 + nrm(24, (DEPTH, HG_VDIM), 0.02),
        'w_branch': nrm(25, (DEPTH, N_BRANCH, MIX_W, D_MODEL), MIX_W ** -0.5 * DEEPNORM_BETA),
        'w_out': nrm(26, (DEPTH, D_MODEL, D_MODEL), D_MODEL ** -0.5 * DEEPNORM_BETA),
        'ln1_g': 1.0 + nrm(27, (DEPTH, D_MODEL), 0.02),
        'ln1_b': nrm(28, (DEPTH, D_MODEL), 0.01),
        'router_group_w': nrm(29, (DEPTH, D_MODEL, N_GROUPS), D_MODEL ** -0.5),
        'router_group_b': nrm(30, (DEPTH, N_GROUPS), 0.01),
        'router_expert_w': nrm(31, (DEPTH, D_MODEL, N_EXPERTS), D_MODEL ** -0.5),
        'router_expert_b': nrm(32, (DEPTH, N_EXPERTS), 0.01),
        'exp_w1': nrm(33, (DEPTH, N_EXPERTS, D_MODEL, EXPERT_FF), D_MODEL ** -0.5),
        'exp_w3': nrm(34, (DEPTH, N_EXPERTS, D_MODEL, EXPERT_FF), D_MODEL ** -0.5),
        'exp_w2': nrm(35, (DEPTH, N_EXPERTS, EXPERT_FF, D_MODEL), EXPERT_FF ** -0.5 * DEEPNORM_BETA),
        'ln2_g': 1.0 + nrm(36, (DEPTH, D_MODEL), 0.02),
        'ln2_b': nrm(37, (DEPTH, D_MODEL), 0.01),
    }


def reference(x_prompt, x_sample, cache_diff_k, cache_diff_v, cache_sb_k, cache_sb_v, page_table,
              state_conv, state_lru, state_hgrn, w_in, conv_w, conv_b, lru_wa, lru_ba, lru_wx, lru_bx,
              lru_lambda, diff_lam_q1, diff_lam_k1, diff_lam_q2, diff_lam_k2, diff_norm_g, hgrn_lb_raw,
              hgrn_norm_g, w_branch, w_out, ln1_g, ln1_b, router_group_w, router_group_b,
              router_expert_w, router_expert_b, exp_w1, exp_w3, exp_w2, ln2_g, ln2_b):
    lb_soft = jax.nn.softmax(hgrn_lb_raw.astype(jnp.float32), axis=0)
    hg_lb = jnp.clip(jnp.cumsum(lb_soft, axis=0) - lb_soft[0:1], 0.0, 1.0)

    past_len = page_table.shape[1] * PAGE_SIZE
    pos_p = jnp.arange(x_prompt.shape[1], dtype=jnp.int32)
    pos_s = past_len + jnp.arange(x_sample.shape[1], dtype=jnp.int32)

    xp, xs = x_prompt, x_sample
    states_p, states_s = [], []
    for l in range(DEPTH):
        p = dict(w_in=w_in[l], conv_w=conv_w[l], conv_b=conv_b[l], lru_wa=lru_wa[l], lru_ba=lru_ba[l],
                 lru_wx=lru_wx[l], lru_bx=lru_bx[l], lru_lambda=lru_lambda[l],
                 diff_lam_q1=diff_lam_q1[l], diff_lam_k1=diff_lam_k1[l],
                 diff_lam_q2=diff_lam_q2[l], diff_lam_k2=diff_lam_k2[l], diff_norm_g=diff_norm_g[l],
                 hgrn_norm_g=hgrn_norm_g[l], w_branch=w_branch[l], w_out=w_out[l],
                 ln1_g=ln1_g[l], ln1_b=ln1_b[l], router_group_w=router_group_w[l],
                 router_group_b=router_group_b[l], router_expert_w=router_expert_w[l],
                 router_expert_b=router_expert_b[l], exp_w1=exp_w1[l], exp_w3=exp_w3[l],
                 exp_w2=exp_w2[l], ln2_g=ln2_g[l], ln2_b=ln2_b[l])
        past = dict(diff_k=gather_pages(cache_diff_k[l], page_table),
                    diff_v=gather_pages(cache_diff_v[l], page_table),
                    sb_k=gather_pages(cache_sb_k[l], page_table),
                    sb_v=gather_pages(cache_sb_v[l], page_table),
                    conv=state_conv[l], lru=state_lru[l], hgrn=state_hgrn[l])
        xp, st_p = trunk_layer(xp, pos_p, p, l, hg_lb[l], None)
        xs, st_s = trunk_layer(xs, pos_s, p, l, hg_lb[l], past)
        states_p.append(st_p)
        states_s.append(st_s)

    diff_k_p, diff_v_p, sb_k_p, sb_v_p, conv_p, lru_p, hgrn_p = [jnp.stack(s, axis=0) for s in zip(*states_p)]
    diff_k_s, diff_v_s, sb_k_s, sb_v_s, conv_s, lru_s, hgrn_s = [jnp.stack(s, axis=0) for s in zip(*states_s)]
    return (xp, xs, diff_k_p, diff_v_p, sb_k_p, sb_v_p, conv_p, lru_p, hgrn_p,
            diff_k_s, diff_v_s, sb_k_s, sb_v_s, conv_s, lru_s, hgrn_s)
```

```python
import functools
import math

import jax
import jax.numpy as jnp
from jax import lax
from jax.experimental import pallas as pl
from jax.experimental.pallas import tpu as pltpu

F32 = jnp.float32
BF16 = jnp.bfloat16

N_BRANCH = 4
MIX_W = 256
N_HEADS = 4
HEAD_DIM = 64
DIFF_QK_DIM = 32
CONV_WIDTH = 4
LRU_C = 8.0
N_GROUPS = 4
EXPERTS_PER_GROUP = 4
N_EXPERTS = 16
LN_EPS = 1e-5
RMS_EPS = 1e-5
NEG_BIG = -1e30
F_FLOOR = 1e-30
PAGE_SIZE = 128

COL_XA, COL_GA, COL_DQ, COL_DK, COL_DV, COL_SQ, COL_SK, COL_SV, COL_HQ, COL_HF, COL_HI, COL_HG = range(12)
N_MIX_COLS = 12

V7X_VMEM_BYTES = 64 * 1024 * 1024
VMEM_LIMIT = V7X_VMEM_BYTES - 12 * 1024 * 1024

HG_CHUNK = 64
HG_SUB = 16
ROUTER_LANES = 128
ROUTER_E0 = 16


def _cparams(*sem):
    return pltpu.CompilerParams(dimension_semantics=sem, vmem_limit_bytes=VMEM_LIMIT)


def _dot(a, b):
    return jnp.dot(a, b, preferred_element_type=F32)


def _dot_nt(a, b):
    return lax.dot_general(a, b, (((1,), (1,)), ((), ())), preferred_element_type=F32)


def _dot_tn(a, b):
    return lax.dot_general(a, b, (((0,), (0,)), ((), ())), preferred_element_type=F32)


def _split_dot(x, w_bf16):
    hi = x.astype(BF16)
    lo = (x - hi.astype(F32)).astype(BF16)
    return _dot(hi, w_bf16) + _dot(lo, w_bf16)


def _sigmoid(x):
    return 1.0 / (1.0 + jnp.exp(-x))


def _silu(x):
    return x * _sigmoid(x)


def _gelu_tanh(x):
    c = math.sqrt(2.0 / math.pi)
    return 0.5 * x * (1.0 + jnp.tanh(c * (x + 0.044715 * (x * x * x))))


def _softplus(x):
    return jnp.maximum(x, 0.0) + jnp.log(1.0 + jnp.exp(-jnp.abs(x)))


def _iota(shape, dim):
    return lax.broadcasted_iota(jnp.int32, shape, dim)


def _head_ones(n):
    return jnp.where((_iota((n, n), 0) // HEAD_DIM) == (_iota((n, n), 1) // HEAD_DIM), 1.0, 0.0)


def _layer_norm(h, g, b):
    mu = jnp.mean(h, axis=-1, keepdims=True)
    d = h - mu
    var = jnp.mean(d * d, axis=-1, keepdims=True)
    return d * lax.rsqrt(var + LN_EPS) * g + b


def _lb_kernel(raw_ref, o_ref):
    raw = raw_ref[...]
    m = jnp.max(raw, axis=0, keepdims=True)
    e = jnp.exp(raw - m)
    soft = e / jnp.sum(e, axis=0, keepdims=True)
    rows, run = [], jnp.zeros_like(soft[0:1, :])
    for l in range(raw.shape[0]):
        run = run + soft[l:l + 1, :]
        rows.append(run)
    cum = jnp.concatenate(rows, axis=0)
    o_ref[...] = jnp.clip(cum - soft[0:1, :], 0.0, 1.0)


def _hgrn_lower_bounds(raw):
    return pl.pallas_call(_lb_kernel, out_shape=jax.ShapeDtypeStruct(raw.shape, F32))(raw)


def _mm_kernel(x_ref, w_ref, o_ref):
    o_ref[...] = _dot(x_ref[...].astype(BF16), w_ref[...])


def _in_proj(x, w):
    n, k = x.shape
    c = w.shape[1]
    tm = min(512, n)
    tn = 1024
    return pl.pallas_call(
        _mm_kernel,
        grid=(n // tm, c // tn),
        in_specs=[pl.BlockSpec((tm, k), lambda i, j: (i, 0)),
                  pl.BlockSpec((k, tn), lambda i, j: (0, j))],
        out_specs=pl.BlockSpec((tm, tn), lambda i, j: (i, j)),
        out_shape=jax.ShapeDtypeStruct((n, c), F32),
        compiler_params=_cparams("parallel", "parallel"),
    )(x, w)


def _lru_gates(xc, wa, ba, wx, bx, lam):
    xcb = xc.astype(BF16)
    r = _sigmoid(_dot(xcb, wa) + ba)
    i_g = _sigmoid(_dot(xcb, wx) + bx)
    log_a = -LRU_C * r * _softplus(-lam)
    a = jnp.exp(log_a)
    mult = jnp.sqrt(jnp.maximum(1.0 - jnp.exp(2.0 * log_a), 0.0))
    return a, mult, i_g


def _rglru_prompt_kernel(xa_ref, ga_ref, cw_ref, cb_ref, wa_ref, ba_ref, wx_ref, bx_ref, lam_ref,
                         y_ref, conv_ref, h_ref, xbuf, sa, sb, hc):
    t = pl.program_id(1)
    tt = xa_ref.shape[0]
    pad = tt // 2

    @pl.when(t == 0)
    def _():
        xbuf[0:8, :] = jnp.zeros((8, MIX_W), F32)
        hc[...] = jnp.zeros_like(hc)

    sa[0:pad, :] = jnp.ones((pad, MIX_W), F32)
    sb[0:pad, :] = jnp.zeros((pad, MIX_W), F32)

    xa = xa_ref[...]
    xbuf[8:8 + tt, :] = xa
    xc = cb_ref[...] + cw_ref[CONV_WIDTH - 1:CONV_WIDTH, :] * xa
    for i in range(CONV_WIDTH - 1):
        xc = xc + cw_ref[i:i + 1, :] * xbuf[5 + i:5 + i + tt, :]
    a, mult, i_g = _lru_gates(xc, wa_ref[...], ba_ref[...], wx_ref[...], bx_ref[...], lam_ref[...])
    pos = _iota((tt, 1), 0) + t * tt
    mult = jnp.where(pos == 0, 1.0, mult)
    b = mult * i_g * xc
    sa[pad:pad + tt, :] = a
    sb[pad:pad + tt, :] = b
    sb[pad:pad + 1, :] = b[0:1, :] + a[0:1, :] * hc[...]

    d = 1
    while d < tt:
        a_cur = sa[pad:pad + tt, :]
        b_cur = sb[pad:pad + tt, :]
        a_sh = sa[pad - d:pad - d + tt, :]
        b_sh = sb[pad - d:pad - d + tt, :]
        sb[pad:pad + tt, :] = a_cur * b_sh + b_cur
        if 2 * d < tt:
            sa[pad:pad + tt, :] = a_cur * a_sh
        d *= 2

    h = sb[pad:pad + tt, :]
    y_ref[...] = _gelu_tanh(ga_ref[...]) * h
    hc[...] = h[tt - 1:tt, :]
    xbuf[0:8, :] = xa[tt - 8:tt, :]
    conv_ref[...] = xa[tt - (CONV_WIDTH - 1):tt, :]
    h_ref[...] = h[tt - 1:tt, :]


def _rglru_prompt(u3, cw, cb, wa, ba, wx, bx, lam, tt=512):
    bsz, t, _ = u3.shape
    full = lambda shape: pl.BlockSpec(shape, lambda b, i: (0,) * len(shape))
    return pl.pallas_call(
        _rglru_prompt_kernel,
        grid=(bsz, t // tt),
        in_specs=[pl.BlockSpec((None, tt, MIX_W), lambda b, i: (b, i, COL_XA)),
                  pl.BlockSpec((None, tt, MIX_W), lambda b, i: (b, i, COL_GA)),
                  full((CONV_WIDTH, MIX_W)), full((1, MIX_W)), full((MIX_W, MIX_W)), full((1, MIX_W)),
                  full((MIX_W, MIX_W)), full((1, MIX_W)), full((1, MIX_W))],
        out_specs=[pl.BlockSpec((None, tt, MIX_W), lambda b, i: (b, i, 0)),
                   pl.BlockSpec((None, CONV_WIDTH - 1, MIX_W), lambda b, i: (b, 0, 0)),
                   pl.BlockSpec((None, 1, MIX_W), lambda b, i: (b, 0, 0))],
        out_shape=[jax.ShapeDtypeStruct((bsz, t, MIX_W), F32),
                   jax.ShapeDtypeStruct((bsz, CONV_WIDTH - 1, MIX_W), F32),
                   jax.ShapeDtypeStruct((bsz, 1, MIX_W), F32)],
        scratch_shapes=[pltpu.VMEM((tt + 8, MIX_W), F32),
                        pltpu.VMEM((tt + tt // 2, MIX_W), F32),
                        pltpu.VMEM((tt + tt // 2, MIX_W), F32),
                        pltpu.VMEM((1, MIX_W), F32)],
        compiler_params=_cparams("parallel", "arbitrary"),
    )(u3, u3, cw, cb, wa, ba, wx, bx, lam)


def _rglru_decode_kernel(xa_ref, ga_ref, conv_ref, h0_ref, cw_ref, cb_ref, wa_ref, ba_ref, wx_ref, bx_ref,
                         lam_ref, y_ref, convn_ref, h_ref):
    xa = xa_ref[...]
    w = MIX_W
    xc = cb_ref[...] + cw_ref[CONV_WIDTH - 1:CONV_WIDTH, :] * xa
    for i in range(CONV_WIDTH - 1):
        xc = xc + cw_ref[i:i + 1, :] * conv_ref[:, i * w:(i + 1) * w]
    a, mult, i_g = _lru_gates(xc, wa_ref[...], ba_ref[...], wx_ref[...], bx_ref[...], lam_ref[...])
    h = a * h0_ref[...] + mult * i_g * xc
    y_ref[...] = _gelu_tanh(ga_ref[...]) * h
    h_ref[...] = h
    convn_ref[:, 0:(CONV_WIDTH - 2) * w] = conv_ref[:, w:(CONV_WIDTH - 1) * w]
    convn_ref[:, (CONV_WIDTH - 2) * w:(CONV_WIDTH - 1) * w] = xa


def _rglru_decode(u, conv, h0, cw, cb, wa, ba, wx, bx, lam):
    n = u.shape[0]
    cwid = (CONV_WIDTH - 1) * MIX_W
    full = lambda shape: pl.BlockSpec(shape, lambda i: (0,) * len(shape))
    return pl.pallas_call(
        _rglru_decode_kernel,
        grid=(1,),
        in_specs=[pl.BlockSpec((n, MIX_W), lambda i: (0, COL_XA)),
                  pl.BlockSpec((n, MIX_W), lambda i: (0, COL_GA)),
                  full((n, cwid)), full((n, MIX_W)),
                  full((CONV_WIDTH, MIX_W)), full((1, MIX_W)), full((MIX_W, MIX_W)), full((1, MIX_W)),
                  full((MIX_W, MIX_W)), full((1, MIX_W)), full((1, MIX_W))],
        out_specs=[full((n, MIX_W)), full((n, cwid)), full((n, MIX_W))],
        out_shape=[jax.ShapeDtypeStruct((n, MIX_W), F32),
                   jax.ShapeDtypeStruct((n, cwid), F32),
                   jax.ShapeDtypeStruct((n, MIX_W), F32)],
        compiler_params=_cparams("arbitrary"),
    )(u, u, conv, h0, cw, cb, wa, ba, wx, bx, lam)


def _alibi_slope(h):
    return 2.0 ** (-8.0 * (h + 1) / N_HEADS)


def _diff_lambda(lam_ref, cst_ref):
    lv = lam_ref[...]
    s1 = jnp.sum(lv[0:1, :] * lv[1:2, :], axis=1, keepdims=True)
    s2 = jnp.sum(lv[2:3, :] * lv[3:4, :], axis=1, keepdims=True)
    lam_init = cst_ref[0:1, 0:1]
    return jnp.exp(s1) - jnp.exp(s2) + lam_init, lam_init


def _diff_q_rows(q, h):
    lane = _iota((1, MIX_W), 1)
    lo = h * HEAD_DIM
    qs = q * (DIFF_QK_DIM ** -0.5)
    q1 = jnp.where((lane >= lo) & (lane < lo + DIFF_QK_DIM), qs, 0.0)
    q2 = jnp.where((lane >= lo + DIFF_QK_DIM) & (lane < lo + HEAD_DIM), qs, 0.0)
    return jnp.concatenate([q1, q2], axis=0).astype(BF16)


def _diff_prompt_kernel(q_ref, k_ref, v_ref, lam_ref, cst_ref, g_ref, y_ref, m_s, l_s, acc_s):
    i = pl.program_id(1)
    tq = q_ref.shape[0]
    tk = tq
    q = q_ref[...]
    rel = (_iota((2 * tq, tk), 0) % tq - _iota((2 * tq, tk), 1)).astype(F32)
    lam, lam_init = _diff_lambda(lam_ref, cst_ref)
    outs = []
    for h in range(N_HEADS):
        slope = _alibi_slope(h)
        qs = _diff_q_rows(q, h)
        bias = rel * (-slope)
        m_s[...] = jnp.full(m_s.shape, NEG_BIG, F32)
        l_s[...] = jnp.zeros_like(l_s)
        acc_s[...] = jnp.zeros_like(acc_s)

        def block(j, masked, qs=qs, bias=bias, slope=slope, h=h):
            k0 = pl.multiple_of(j * tk, tk)
            kb = k_ref[pl.ds(k0, tk), :].astype(BF16)
            vb = v_ref[pl.ds(k0, tk), :][:, h * HEAD_DIM:(h + 1) * HEAD_DIM].astype(BF16)
            s = _dot_nt(qs, kb) + bias
            if masked:
                s = jnp.where(rel >= 0.0, s, NEG_BIG)
            cj = ((i - j) * tq).astype(F32) * (-slope)
            m_prev = m_s[...]
            m_new = jnp.maximum(m_prev, jnp.max(s, axis=1, keepdims=True) + cj)
            p = jnp.exp(s - (m_new - cj))
            alpha = jnp.exp(m_prev - m_new)
            l_s[...] = alpha * l_s[...] + jnp.sum(p, axis=1, keepdims=True)
            acc_s[...] = alpha * acc_s[...] + _dot(p.astype(BF16), vb)
            m_s[...] = m_new

        def body(j, carry):
            block(j, False)
            return carry

        lax.fori_loop(0, i, body, 0)
        block(i, True)
        o = acc_s[...] / l_s[...]
        o = o[0:tq, :] - lam * o[tq:2 * tq, :]
        inv = lax.rsqrt(jnp.mean(o * o, axis=1, keepdims=True) + RMS_EPS)
        outs.append(o * inv * g_ref[...] * (1.0 - lam_init))
    y_ref[...] = jnp.concatenate(outs, axis=1)


def _diff_prompt(u3, lam_vecs, consts, norm_g, tq=256):
    bsz, t, _ = u3.shape
    full = lambda shape: pl.BlockSpec(shape, lambda b, i: (0,) * len(shape))
    return pl.pallas_call(
        _diff_prompt_kernel,
        grid=(bsz, t // tq),
        in_specs=[pl.BlockSpec((None, tq, MIX_W), lambda b, i: (b, i, COL_DQ)),
                  pl.BlockSpec((None, t, MIX_W), lambda b, i: (b, 0, COL_DK)),
                  pl.BlockSpec((None, t, MIX_W), lambda b, i: (b, 0, COL_DV)),
                  full(lam_vecs.shape), full(consts.shape), full(norm_g.shape)],
        out_specs=pl.BlockSpec((None, tq, MIX_W), lambda b, i: (b, i, 0)),
        out_shape=jax.ShapeDtypeStruct((bsz, t, MIX_W), F32),
        scratch_shapes=[pltpu.VMEM((2 * tq, 1), F32), pltpu.VMEM((2 * tq, 1), F32),
                        pltpu.VMEM((2 * tq, HEAD_DIM), F32)],
        compiler_params=_cparams("parallel", "arbitrary"),
    )(u3, u3, u3, lam_vecs, consts, norm_g)


def _sb_q_rows(q, h):
    lane = _iota((1, MIX_W), 1)
    lo = h * HEAD_DIM
    return jnp.where((lane >= lo) & (lane < lo + HEAD_DIM), q * (HEAD_DIM ** -0.5), 0.0).astype(BF16)


def _log_sigmoids(z):
    ls_neg = -(jnp.maximum(z, 0.0) + jnp.log(1.0 + jnp.exp(-jnp.abs(z))))
    return ls_neg + z, ls_neg


def _sb_prompt_kernel(q_ref, k_ref, v_ref, y_ref, r_s, acc_s):
    i = pl.program_id(1)
    tq = q_ref.shape[0]
    tk = tq
    q = q_ref[...]
    earlier = _iota((tq, tk), 1) < _iota((tq, tk), 0)
    upper = jnp.where(_iota((tk, tk), 0) > _iota((tk, tk), 1), 1.0, 0.0).astype(BF16)
    outs = []
    for h in range(N_HEADS):
        qs = _sb_q_rows(q, h)
        r_s[...] = jnp.zeros_like(r_s)
        acc_s[...] = jnp.zeros_like(acc_s)

        def block(j, masked, qs=qs, h=h):
            k0 = pl.multiple_of(j * tk, tk)
            kb = k_ref[pl.ds(k0, tk), :].astype(BF16)
            vb = v_ref[pl.ds(k0, tk), :][:, h * HEAD_DIM:(h + 1) * HEAD_DIM].astype(BF16)
            z = _dot_nt(qs, kb)
            ls_pos, ls_neg = _log_sigmoids(z)
            if masked:
                ls_neg = jnp.where(earlier, ls_neg, 0.0)
            later = _split_dot(ls_neg, upper)
            w = jnp.exp(jnp.minimum(ls_pos + later + r_s[...], 0.0))
            if masked:
                w = jnp.where(earlier, w, 0.0)
            acc_s[...] = acc_s[...] + _dot(w.astype(BF16), vb)
            r_s[...] = r_s[...] + later[:, 0:1] + ls_neg[:, 0:1]

        block(i, True)

        def body(jj, carry):
            block(i - 1 - jj, False)
            return carry

        lax.fori_loop(0, i, body, 0)
        outs.append(acc_s[...])
    y_ref[...] = jnp.concatenate(outs, axis=1)


def _sb_prompt(u3, tq=256):
    bsz, t, _ = u3.shape
    return pl.pallas_call(
        _sb_prompt_kernel,
        grid=(bsz, t // tq),
        in_specs=[pl.BlockSpec((None, tq, MIX_W), lambda b, i: (b, i, COL_SQ)),
                  pl.BlockSpec((None, t, MIX_W), lambda b, i: (b, 0, COL_SK)),
                  pl.BlockSpec((None, t, MIX_W), lambda b, i: (b, 0, COL_SV))],
        out_specs=pl.BlockSpec((None, tq, MIX_W), lambda b, i: (b, i, 0)),
        out_shape=jax.ShapeDtypeStruct((bsz, t, MIX_W), F32),
        scratch_shapes=[pltpu.VMEM((tq, 1), F32), pltpu.VMEM((tq, HEAD_DIM), F32)],
        compiler_params=_cparams("parallel", "arbitrary"),
    )(u3, u3, u3)


def _decode_attn_kernel(pt_ref, *refs, n_pages):
    del pt_ref
    np_ = n_pages
    dk_refs = refs[0:np_]
    dv_refs = refs[np_:2 * np_]
    sk_refs = refs[2 * np_:3 * np_]
    sv_refs = refs[3 * np_:4 * np_]
    (dq_ref, dkn_ref, dvn_ref, sq_ref, lam_ref, cst_ref, g_ref, yb_ref, yc_ref) = refs[4 * np_:]
    past = np_ * PAGE_SIZE
    w = MIX_W
    lane = _iota((1, w), 1)
    row8 = _iota((2 * N_HEADS, 1), 0)

    q = dq_ref[...] * (DIFF_QK_DIM ** -0.5)
    seg = lane // DIFF_QK_DIM
    qrows = jnp.where(seg == row8, q, 0.0)
    qb = qrows.astype(BF16)
    slope = jnp.exp((row8 // 2 + 1).astype(F32) * (-8.0 * math.log(2.0) / N_HEADS))
    s_pages = []
    for p in range(np_):
        kpos = (_iota((1, PAGE_SIZE), 1) + p * PAGE_SIZE).astype(F32)
        s = _dot_nt(qb, dk_refs[p][...].astype(BF16))
        s_pages.append(s - slope * (float(past) - kpos))
    s_new = jnp.sum(qrows * dkn_ref[...], axis=1, keepdims=True)
    m = s_new
    for s in s_pages:
        m = jnp.maximum(m, jnp.max(s, axis=1, keepdims=True))
    p_new = jnp.exp(s_new - m)
    l = p_new
    acc = p_new * dvn_ref[...]
    for p in range(np_):
        pr = jnp.exp(s_pages[p] - m)
        l = l + jnp.sum(pr, axis=1, keepdims=True)
        acc = acc + _dot(pr.astype(BF16), dv_refs[p][...].astype(BF16))
    o = acc / l
    lam, lam_init = _diff_lambda(lam_ref, cst_ref)
    coef = jnp.where(row8 % 2 == 0, 1.0, -lam)
    head_of_lane = lane // HEAD_DIM
    o = jnp.where(head_of_lane == row8 // 2, o * coef, 0.0)
    o = jnp.sum(o, axis=0, keepdims=True)
    ms = _split_dot(o * o, _head_ones(w).astype(BF16)) * (1.0 / HEAD_DIM)
    yb_ref[...] = o * lax.rsqrt(ms + RMS_EPS) * g_ref[...] * (1.0 - lam_init)

    rowh = _iota((N_HEADS, 1), 0)
    qsb = jnp.where(head_of_lane == rowh, sq_ref[...] * (HEAD_DIM ** -0.5), 0.0).astype(BF16)
    upper = jnp.where(_iota((PAGE_SIZE, PAGE_SIZE), 0) > _iota((PAGE_SIZE, PAGE_SIZE), 1), 1.0, 0.0).astype(BF16)
    run = jnp.zeros((N_HEADS, 1), F32)
    acc = jnp.zeros((N_HEADS, w), F32)
    for p in range(np_ - 1, -1, -1):
        z = _dot_nt(qsb, sk_refs[p][...].astype(BF16))
        ls_pos, ls_neg = _log_sigmoids(z)
        later = _split_dot(ls_neg, upper)
        wgt = jnp.exp(jnp.minimum(ls_pos + later + run, 0.0))
        acc = acc + _dot(wgt.astype(BF16), sv_refs[p][...].astype(BF16))
        run = run + jnp.sum(ls_neg, axis=1, keepdims=True)
    yc_ref[...] = jnp.sum(jnp.where(head_of_lane == rowh, acc, 0.0), axis=0, keepdims=True)


def _decode_attn(layer, page_table, caches, u3, lam_vecs, consts, norm_g_tiled):
    n, n_pages = page_table.shape

    def page_spec(p):
        return pl.BlockSpec((None, None, PAGE_SIZE, MIX_W), lambda b, pt: (layer, pt[b, p], 0, 0))

    def col_spec(c):
        return pl.BlockSpec((None, 1, MIX_W), lambda b, pt: (b, 0, c))

    full = lambda shape: pl.BlockSpec(shape, lambda b, pt: (0,) * len(shape))
    in_specs = [page_spec(p) for _ in range(4) for p in range(n_pages)]
    in_specs += [col_spec(COL_DQ), col_spec(COL_DK), col_spec(COL_DV), col_spec(COL_SQ),
                 full(lam_vecs.shape), full(consts.shape), full(norm_g_tiled.shape)]
    operands = [c for c in caches for _ in range(n_pages)] + [u3, u3, u3, u3, lam_vecs, consts, norm_g_tiled]
    out_spec = pl.BlockSpec((None, 1, MIX_W), lambda b, pt: (b, 0, 0))
    return pl.pallas_call(
        functools.partial(_decode_attn_kernel, n_pages=n_pages),
        grid_spec=pltpu.PrefetchScalarGridSpec(
            num_scalar_prefetch=1, grid=(n,), in_specs=in_specs, out_specs=[out_spec, out_spec]),
        out_shape=[jax.ShapeDtypeStruct((n, 1, MIX_W), F32), jax.ShapeDtypeStruct((n, 1, MIX_W), F32)],
        compiler_params=_cparams("parallel"),
    )(page_table, *operands)


def _hgrn_gates(hq, hf, lb):
    q = _silu(hq)
    e = jnp.exp(-jnp.abs(hf))
    inv = 1.0 / (1.0 + e)
    pos = hf >= 0.0
    sig = jnp.where(pos, inv, e * inv)
    sig_n = jnp.where(pos, e * inv, inv)
    f = lb + (1.0 - lb) * sig
    return q, f, (1.0 - lb) * sig_n


def _hgrn_out(o, hg, g, ones_bf16):
    ms = _split_dot(o * o, ones_bf16) * (1.0 / HEAD_DIM)
    return o * lax.rsqrt(ms + RMS_EPS) * g * _silu(hg)


def _hgrn_prompt_kernel(hq_ref, hf_ref, hi_ref, hg_ref, lb_ref, g_ref, y_ref, st_ref, st_s):
    t = pl.program_id(1)
    tt = hq_ref.shape[0]
    c = HG_CHUNK
    w = MIX_W

    @pl.when(t == 0)
    def _():
        st_s[...] = jnp.zeros_like(st_s)

    lb = lb_ref[...]
    ones_f = _head_ones(w)
    ones_b = ones_f.astype(BF16)
    tril = jnp.where(_iota((c, c), 1) <= _iota((c, c), 0), 1.0, 0.0).astype(BF16)
    row = _iota((c, 1), 0)
    for ci in range(tt // c):
        sl = slice(ci * c, (ci + 1) * c)
        q, f, k = _hgrn_gates(hq_ref[sl, :], hf_ref[sl, :], lb)
        v = hi_ref[sl, :]
        cum = _split_dot_left(tril, jnp.log(jnp.maximum(f, F_FLOOR)))
        st = st_s[...]
        o = _dot_nt((q * jnp.exp(cum)).astype(BF16), st.astype(BF16))
        for j in range(c // HG_SUB - 1):
            s0, s1 = j * HG_SUB, (j + 1) * HG_SUB
            ref_row = cum[s1 - 1:s1, :]
            qj = q * jnp.exp(jnp.minimum(cum - ref_row, 0.0))
            kj = k[s0:s1, :] * jnp.exp(ref_row - cum[s0:s1, :])
            mt = _dot_tn(v[s0:s1, :].astype(BF16), kj.astype(BF16)) * ones_f
            oj = _dot_nt(qj.astype(BF16), mt.astype(BF16))
            o = o + jnp.where(row >= s1, oj, 0.0)
        for lag in range(HG_SUB):
            if lag == 0:
                ks, cs, vs = k, cum, v
            else:
                ks = pltpu.roll(k, lag, 0)
                cs = pltpu.roll(cum, lag, 0)
                vs = pltpu.roll(v, lag, 0)
            term = q * ks * jnp.exp(jnp.minimum(cum - cs, 0.0))
            ssum = _dot(term.astype(BF16), ones_b)
            o = o + jnp.where(row % HG_SUB >= lag, ssum * vs, 0.0)
        last = cum[c - 1:c, :]
        kc = k * jnp.exp(last - cum)
        st_s[...] = st * jnp.exp(last) + _dot_tn(v.astype(BF16), kc.astype(BF16)) * ones_f
        y_ref[sl, :] = _hgrn_out(o, hg_ref[sl, :], g_ref[...], ones_b)
    st_ref[...] = st_s[...]


def _split_dot_left(w_bf16, x):
    hi = x.astype(BF16)
    lo = (x - hi.astype(F32)).astype(BF16)
    return _dot(w_bf16, hi) + _dot(w_bf16, lo)


def _hgrn_prompt(u3, lb, g_tiled, tt=256):
    bsz, t, _ = u3.shape
    full = lambda shape: pl.BlockSpec(shape, lambda b, i: (0,) * len(shape))
    col = lambda c: pl.BlockSpec((None, tt, MIX_W), lambda b, i: (b, i, c))
    return pl.pallas_call(
        _hgrn_prompt_kernel,
        grid=(bsz, t // tt),
        in_specs=[col(COL_HQ), col(COL_HF), col(COL_HI), col(COL_HG), full((1, MIX_W)), full((1, MIX_W))],
        out_specs=[pl.BlockSpec((None, tt, MIX_W), lambda b, i: (b, i, 0)),
                   pl.BlockSpec((None, MIX_W, MIX_W), lambda b, i: (b, 0, 0))],
        out_shape=[jax.ShapeDtypeStruct((bsz, t, MIX_W), F32),
                   jax.ShapeDtypeStruct((bsz, MIX_W, MIX_W), F32)],
        scratch_shapes=[pltpu.VMEM((MIX_W, MIX_W), F32)],
        compiler_params=_cparams("parallel", "arbitrary"),
    )(u3, u3, u3, u3, lb, g_tiled)


def _hgrn_decode_kernel(s0_ref, q_ref, f_ref, v_ref, lb_ref, sn_ref, o_ref):
    bb = s0_ref.shape[0]
    q, f, k = _hgrn_gates(q_ref[...], f_ref[...], lb_ref[...])
    sn = f * s0_ref[...] + k * v_ref[...]
    sn_ref[...] = sn
    o_ref[...] = jnp.sum((q * sn).reshape(bb, N_HEADS, HEAD_DIM, HEAD_DIM), axis=2)


def _hgrn_decode(s0, q_e, f_e, v_e, lb_e, bb=8):
    n = s0.shape[0]
    blk = pl.BlockSpec((bb, MIX_W, HEAD_DIM), lambda i: (i, 0, 0))
    return pl.pallas_call(
        _hgrn_decode_kernel,
        grid=(n // bb,),
        in_specs=[blk, blk, blk, blk, pl.BlockSpec((1, MIX_W, HEAD_DIM), lambda i: (0, 0, 0))],
        out_specs=[blk, pl.BlockSpec((bb, N_HEADS, HEAD_DIM), lambda i: (i, 0, 0))],
        out_shape=[jax.ShapeDtypeStruct((n, MIX_W, HEAD_DIM), F32),
                   jax.ShapeDtypeStruct((n, N_HEADS, HEAD_DIM), F32)],
        compiler_params=_cparams("parallel"),
    )(s0, q_e, f_e, v_e, lb_e)


def _hgrn_out_kernel(o_ref, hg_ref, g_ref, y_ref):
    y_ref[...] = _hgrn_out(o_ref[...], hg_ref[...], g_ref[...], _head_ones(MIX_W).astype(BF16))


def _hgrn_decode_out(o, u, g_tiled):
    n = o.shape[0]
    full = lambda shape: pl.BlockSpec(shape, lambda i: (0,) * len(shape))
    return pl.pallas_call(
        _hgrn_out_kernel,
        grid=(1,),
        in_specs=[full((n, MIX_W)), pl.BlockSpec((n, MIX_W), lambda i: (0, COL_HG)), full((1, MIX_W))],
        out_specs=full((n, MIX_W)),
        out_shape=jax.ShapeDtypeStruct((n, MIX_W), F32),
        compiler_params=_cparams("arbitrary"),
    )(o, u, g_tiled)


def _merge_kernel(ya_ref, yb_ref, yc_ref, yd_ref, g0_ref, g1_ref, g2_ref, g3_ref, x_ref, wb_ref, wo_ref,
                  lg_ref, lbias_ref, o_ref, *, alpha):
    merged = None
    for y_ref, gl_ref, n in ((ya_ref, g0_ref, 0), (yb_ref, g1_ref, 1), (yc_ref, g2_ref, 2), (yd_ref, g3_ref, 3)):
        term = _sigmoid(gl_ref[...]) * _dot(y_ref[...].astype(BF16), wb_ref[n])
        merged = term if merged is None else merged + term
    out = _dot(merged.astype(BF16), wo_ref[...])
    o_ref[...] = _layer_norm(alpha * x_ref[...] + out, lg_ref[...], lbias_ref[...])


def _merge(ya, yb, yc, yd, u, x, wb, wo, ln_g, ln_b, alpha):
    n, d = x.shape
    tm = min(256, n)
    gate0 = N_MIX_COLS * MIX_W // d
    row = lambda width, c=0: pl.BlockSpec((tm, width), lambda i, c=c: (i, c))
    full = lambda shape: pl.BlockSpec(shape, lambda i: (0,) * len(shape))
    return pl.pallas_call(
        functools.partial(_merge_kernel, alpha=alpha),
        grid=(n // tm,),
        in_specs=[row(MIX_W), row(MIX_W), row(MIX_W), row(MIX_W),
                  row(d, gate0), row(d, gate0 + 1), row(d, gate0 + 2), row(d, gate0 + 3),
                  row(d), full(wb.shape), full(wo.shape), full((1, d)), full((1, d))],
        out_specs=row(d),
        out_shape=jax.ShapeDtypeStruct((n, d), F32),
        compiler_params=_cparams("parallel"),
    )(ya, yb, yc, yd, u, u, u, u, x, wb, wo, ln_g, ln_b)


def _router_weights(x, wr_hi, wr_lo, br):
    xh = x.astype(BF16)
    xl = (x - xh.astype(F32)).astype(BF16)
    logits = _dot(xh, wr_hi) + _dot(xh, wr_lo) + _dot(xl, wr_hi) + br
    lane = _iota((1, ROUTER_LANES), 1)
    big = jnp.int32(ROUTER_LANES)
    is_g = lane < N_GROUPS
    gl = jnp.where(is_g, logits, NEG_BIG)
    gmax = jnp.max(gl, axis=1, keepdims=True)
    g_idx = jnp.min(jnp.where(is_g & (gl == gmax), lane, big), axis=1, keepdims=True)
    g_w = 1.0 / jnp.sum(jnp.where(is_g, jnp.exp(gl - gmax), 0.0), axis=1, keepdims=True)
    in_grp = (lane >= ROUTER_E0) & (lane < ROUTER_E0 + N_EXPERTS) & \
             ((lane - ROUTER_E0) // EXPERTS_PER_GROUP == g_idx)
    el = jnp.where(in_grp, logits, NEG_BIG)
    v1 = jnp.max(el, axis=1, keepdims=True)
    i1 = jnp.min(jnp.where(in_grp & (el == v1), lane, big), axis=1, keepdims=True)
    el2 = jnp.where(lane == i1, NEG_BIG, el)
    v2 = jnp.max(el2, axis=1, keepdims=True)
    i2 = jnp.min(jnp.where(in_grp & (lane != i1) & (el2 == v2), lane, big), axis=1, keepdims=True)
    e2 = jnp.exp(v2 - v1)
    w1 = g_w / (1.0 + e2)
    w2 = g_w * e2 / (1.0 + e2)
    return jnp.where(lane == i1, w1, 0.0) + jnp.where(lane == i2, w2, 0.0)


def _moe_kernel(x_ref, wrh_ref, wrl_ref, br_ref, w1_ref, w3_ref, w2_ref, lg_ref, lbias_ref, o_ref,
                comb_s, acc_s, *, alpha):
    e = pl.program_id(1)
    x = x_ref[...]

    @pl.when(e == 0)
    def _():
        comb_s[...] = _router_weights(x, wrh_ref[...], wrl_ref[...], br_ref[...])
        acc_s[...] = jnp.zeros_like(acc_s)

    lane = _iota((1, ROUTER_LANES), 1)
    c_e = jnp.sum(jnp.where(lane == ROUTER_E0 + e, comb_s[...], 0.0), axis=1, keepdims=True)
    xb = x.astype(BF16)
    hid = _silu(_dot(xb, w1_ref[...])) * _dot(xb, w3_ref[...]) * c_e
    acc_s[...] += _dot(hid.astype(BF16), w2_ref[...])

    @pl.when(e == pl.num_programs(1) - 1)
    def _():
        o_ref[...] = _layer_norm(alpha * x + acc_s[...], lg_ref[...], lbias_ref[...])


def _moe(x, wr_hi, wr_lo, br, w1, w3, w2, ln_g, ln_b, alpha):
    n, d = x.shape
    ne, _, ff = w1.shape
    tm = min(1024, n)
    full = lambda shape: pl.BlockSpec(shape, lambda i, e: (0,) * len(shape))
    return pl.pallas_call(
        functools.partial(_moe_kernel, alpha=alpha),
        grid=(n // tm, ne),
        in_specs=[pl.BlockSpec((tm, d), lambda i, e: (i, 0)),
                  full(wr_hi.shape), full(wr_lo.shape), full(br.shape),
                  pl.BlockSpec((None, d, ff), lambda i, e: (e, 0, 0)),
                  pl.BlockSpec((None, d, ff), lambda i, e: (e, 0, 0)),
                  pl.BlockSpec((None, ff, d), lambda i, e: (e, 0, 0)),
                  full((1, d)), full((1, d))],
        out_specs=pl.BlockSpec((tm, d), lambda i, e: (i, 0)),
        out_shape=jax.ShapeDtypeStruct((n, d), F32),
        scratch_shapes=[pltpu.VMEM((tm, ROUTER_LANES), F32), pltpu.VMEM((tm, d), F32)],
        compiler_params=_cparams("parallel", "arbitrary"),
    )(x, wr_hi, wr_lo, br, w1, w3, w2, ln_g, ln_b)


def _block_diag(w):
    nb, n, _ = w.shape
    eye = jnp.eye(nb, dtype=w.dtype)
    return (eye[:, None, :, None] * w[:, :, None, :]).reshape(nb * n, nb * n)


def _router_matrix(wg, bg, we, be):
    d = wg.shape[0]
    wr = jnp.zeros((d, ROUTER_LANES), F32)
    wr = wr.at[:, 0:N_GROUPS].set(wg).at[:, ROUTER_E0:ROUTER_E0 + N_EXPERTS].set(we)
    br = jnp.zeros((1, ROUTER_LANES), F32)
    br = br.at[0, 0:N_GROUPS].set(bg).at[0, ROUTER_E0:ROUTER_E0 + N_EXPERTS].set(be)
    hi = wr.astype(BF16)
    lo = (wr - hi.astype(F32)).astype(BF16)
    return hi, lo, br


def kernel(x_prompt, x_sample, cache_diff_k, cache_diff_v, cache_sb_k, cache_sb_v, page_table, state_conv, state_lru, state_hgrn, w_in, conv_w, conv_b, lru_wa, lru_ba, lru_wx, lru_bx, lru_lambda, diff_lam_q1, diff_lam_k1, diff_lam_q2, diff_lam_k2, diff_norm_g, hgrn_lb_raw, hgrn_norm_g, w_branch, w_out, ln1_g, ln1_b, router_group_w, router_group_b, router_expert_w, router_expert_b, exp_w1, exp_w3, exp_w2, ln2_g, ln2_b):
    depth = w_in.shape[0]
    bsz, seq, d = x_prompt.shape
    nd = x_sample.shape[0]
    n_pool = cache_diff_k.shape[1]
    alpha = (2 * depth) ** 0.25
    row = lambda a: a.reshape(1, -1)

    hg_lb = _hgrn_lower_bounds(hgrn_lb_raw)
    caches = [c.reshape(depth, n_pool, PAGE_SIZE, MIX_W) for c in (cache_diff_k, cache_diff_v, cache_sb_k, cache_sb_v)]

    xp = x_prompt.reshape(bsz * seq, d)
    xs = x_sample.reshape(nd, d)
    outs_p = [[] for _ in range(7)]
    outs_s = [[] for _ in range(7)]
    for l in range(depth):
        w_in_b = w_in[l].astype(BF16)
        wa = _block_diag(lru_wa[l]).astype(BF16)
        wx = _block_diag(lru_wx[l]).astype(BF16)
        lru = (conv_w[l], row(conv_b[l]), wa, row(lru_ba[l]), wx, row(lru_bx[l]), row(lru_lambda[l]))
        lam_vecs = jnp.stack([diff_lam_q1[l], diff_lam_k1[l], diff_lam_q2[l], diff_lam_k2[l]])
        consts = jnp.zeros((1, 128), F32).at[0, 0].set(0.8 - 0.6 * math.exp(-0.3 * l))
        dn_g = row(diff_norm_g[l])
        dn_g_t = jnp.tile(dn_g, (1, N_HEADS))
        hg_g_t = jnp.tile(row(hgrn_norm_g[l]), (1, N_HEADS))
        lb = row(hg_lb[l])
        wb = w_branch[l].astype(BF16)
        wo = w_out[l].astype(BF16)
        wr_hi, wr_lo, br = _router_matrix(router_group_w[l], router_group_b[l], router_expert_w[l], router_expert_b[l])
        w1, w3, w2 = exp_w1[l].astype(BF16), exp_w3[l].astype(BF16), exp_w2[l].astype(BF16)
        merge_w = (wb, wo, row(ln1_g[l]), row(ln1_b[l]))
        moe_w = (wr_hi, wr_lo, br, w1, w3, w2, row(ln2_g[l]), row(ln2_b[l]))

        u = _in_proj(xp, w_in_b)
        u3 = u.reshape(bsz, seq, -1)
        ya, conv_p, lru_p = _rglru_prompt(u3, *lru)
        yb = _diff_prompt(u3, lam_vecs, consts, dn_g)
        yc = _sb_prompt(u3)
        yd, st_t = _hgrn_prompt(u3, lb, hg_g_t)
        flat = lambda a: a.reshape(bsz * seq, MIX_W)
        x1 = _merge(flat(ya), flat(yb), flat(yc), flat(yd), u, xp, *merge_w, alpha)
        xp = _moe(x1, *moe_w, alpha)
        heads = lambda c: u[:, c * MIX_W:(c + 1) * MIX_W].reshape(bsz, seq, N_HEADS, HEAD_DIM)
        st = jnp.stack([st_t[:, h * HEAD_DIM:(h + 1) * HEAD_DIM, h * HEAD_DIM:(h + 1) * HEAD_DIM]
                        for h in range(N_HEADS)], axis=1).swapaxes(-1, -2)
        for lst, val in zip(outs_p, (heads(COL_DK), heads(COL_DV), heads(COL_SK), heads(COL_SV), conv_p,
                                     lru_p.reshape(bsz, MIX_W), st)):
            lst.append(val)

        us = _in_proj(xs, w_in_b)
        ya, conv_s, lru_s = _rglru_decode(us, state_conv[l].reshape(nd, -1), state_lru[l], *lru)
        yb, yc = _decode_attn(l, page_table, caches, us.reshape(nd, 1, -1), lam_vecs, consts, dn_g_t)
        colb = lambda c: jnp.broadcast_to(us[:, c * MIX_W:(c + 1) * MIX_W, None], (nd, MIX_W, HEAD_DIM))
        v_e = jnp.broadcast_to(us[:, COL_HI * MIX_W:(COL_HI + 1) * MIX_W].reshape(nd, N_HEADS, 1, HEAD_DIM),
                               (nd, N_HEADS, HEAD_DIM, HEAD_DIM)).reshape(nd, MIX_W, HEAD_DIM)
        lb_e = jnp.broadcast_to(hg_lb[l][None, :, None], (1, MIX_W, HEAD_DIM))
        hgrn_s, o_d = _hgrn_decode(state_hgrn[l].reshape(nd, MIX_W, HEAD_DIM), colb(COL_HQ), colb(COL_HF), v_e, lb_e)
        yd = _hgrn_decode_out(o_d.reshape(nd, MIX_W), us, hg_g_t)
        x1 = _merge(ya, yb.reshape(nd, MIX_W), yc.reshape(nd, MIX_W), yd, us, xs, *merge_w, alpha)
        xs = _moe(x1, *moe_w, alpha)
        heads = lambda c: us[:, c * MIX_W:(c + 1) * MIX_W].reshape(nd, 1, N_HEADS, HEAD_DIM)
        for lst, val in zip(outs_s, (heads(COL_DK), heads(COL_DV), heads(COL_SK), heads(COL_SV),
                                     conv_s.reshape(nd, CONV_WIDTH - 1, MIX_W), lru_s,
                                     hgrn_s.reshape(nd, N_HEADS, HEAD_DIM, HEAD_DIM))):
            lst.append(val)

    stack = lambda lsts: [jnp.stack(v, axis=0) for v in lsts]
    return (xp.reshape(bsz, seq, d), xs.reshape(nd, 1, d), *stack(outs_p), *stack(outs_s))
```

```python
import functools
import math

import jax
import jax.numpy as jnp
from jax import lax
from jax.experimental import pallas as pl
from jax.experimental.pallas import tpu as pltpu

F32 = jnp.float32
BF16 = jnp.bfloat16

N_BRANCH = 4
MIX_W = 256
N_HEADS = 4
HEAD_DIM = 64
DIFF_QK_DIM = 32
CONV_WIDTH = 4
LRU_C = 8.0
N_GROUPS = 4
EXPERTS_PER_GROUP = 4
N_EXPERTS = 16
LN_EPS = 1e-5
RMS_EPS = 1e-5
NEG_BIG = -1e30
F_FLOOR = 1e-30
PAGE_SIZE = 128
LOG2E = math.log2(math.e)
LANES = 128

COL_XA, COL_GA, COL_DQ, COL_DK, COL_DV, COL_SQ, COL_SK, COL_SV, COL_HQ, COL_HF, COL_HI, COL_HG = range(12)
N_MIX_COLS = 12

V7X_VMEM_BYTES = 64 * 1024 * 1024
VMEM_LIMIT = V7X_VMEM_BYTES - 12 * 1024 * 1024

HG_CHUNK = 64
HG_SUB = 16
ROUTER_LANES = 128
ROUTER_E0 = 16


def _cparams(*sem):
    return pltpu.CompilerParams(dimension_semantics=sem, vmem_limit_bytes=VMEM_LIMIT)


def _dot(a, b):
    return jnp.dot(a, b, preferred_element_type=F32)


def _dot_nt(a, b):
    return lax.dot_general(a, b, (((1,), (1,)), ((), ())), preferred_element_type=F32)


def _dot_tn(a, b):
    return lax.dot_general(a, b, (((0,), (0,)), ((), ())), preferred_element_type=F32)


def _split_dot(x, w_bf16):
    hi = x.astype(BF16)
    lo = (x - hi.astype(F32)).astype(BF16)
    return _dot(hi, w_bf16) + _dot(lo, w_bf16)


def _sigmoid(x):
    return 1.0 / (1.0 + jnp.exp(-x))


def _silu(x):
    return x * _sigmoid(x)


def _gelu_tanh(x):
    c = math.sqrt(2.0 / math.pi)
    return 0.5 * x * (1.0 + jnp.tanh(c * (x + 0.044715 * (x * x * x))))


def _softplus(x):
    return jnp.maximum(x, 0.0) + jnp.log(1.0 + jnp.exp(-jnp.abs(x)))


def _iota(shape, dim):
    return lax.broadcasted_iota(jnp.int32, shape, dim)


def _head_ones(n):
    return jnp.where((_iota((n, n), 0) // HEAD_DIM) == (_iota((n, n), 1) // HEAD_DIM), 1.0, 0.0)


def _layer_norm(h, g, b):
    mu = jnp.mean(h, axis=-1, keepdims=True)
    d = h - mu
    var = jnp.mean(d * d, axis=-1, keepdims=True)
    return d * lax.rsqrt(var + LN_EPS) * g + b


def _lb_kernel(raw_ref, o_ref):
    raw = raw_ref[...]
    m = jnp.max(raw, axis=0, keepdims=True)
    e = jnp.exp(raw - m)
    soft = e / jnp.sum(e, axis=0, keepdims=True)
    rows, run = [], jnp.zeros_like(soft[0:1, :])
    for l in range(raw.shape[0]):
        run = run + soft[l:l + 1, :]
        rows.append(run)
    cum = jnp.concatenate(rows, axis=0)
    o_ref[...] = jnp.clip(cum - soft[0:1, :], 0.0, 1.0)


def _hgrn_lower_bounds(raw):
    return pl.pallas_call(_lb_kernel, out_shape=jax.ShapeDtypeStruct(raw.shape, F32))(raw)


def _mm_kernel(x_ref, w_ref, o_ref):
    o_ref[...] = _dot(x_ref[...].astype(BF16), w_ref[...])


def _in_proj(x, w):
    n, k = x.shape
    c = w.shape[1]
    tm = min(1024, n)
    tn = 1024
    return pl.pallas_call(
        _mm_kernel,
        grid=(n // tm, c // tn),
        in_specs=[pl.BlockSpec((tm, k), lambda i, j: (i, 0)),
                  pl.BlockSpec((k, tn), lambda i, j: (0, j))],
        out_specs=pl.BlockSpec((tm, tn), lambda i, j: (i, j)),
        out_shape=jax.ShapeDtypeStruct((n, c), F32),
        compiler_params=_cparams("parallel", "parallel"),
    )(x, w)


def _lru_gates(xc, wa, ba, wx, bx, lam):
    xcb = xc.astype(BF16)
    r = _sigmoid(_dot(xcb, wa) + ba)
    i_g = _sigmoid(_dot(xcb, wx) + bx)
    log_a = -LRU_C * r * _softplus(-lam)
    a = jnp.exp(log_a)
    mult = jnp.sqrt(jnp.maximum(1.0 - jnp.exp(2.0 * log_a), 0.0))
    return a, mult, i_g


def _rglru_prompt_kernel(xa_ref, ga_ref, cw_ref, cb_ref, wa_ref, ba_ref, wx_ref, bx_ref, lam_ref,
                         y_ref, conv_ref, h_ref, xbuf, sa, sb, hc):
    t = pl.program_id(1)
    tt = xa_ref.shape[0]
    pad = tt // 2

    @pl.when(t == 0)
    def _():
        xbuf[0:8, :] = jnp.zeros((8, MIX_W), F32)
        hc[...] = jnp.zeros_like(hc)

    sa[0:pad, :] = jnp.ones((pad, MIX_W), F32)
    sb[0:pad, :] = jnp.zeros((pad, MIX_W), F32)

    xa = xa_ref[...]
    xbuf[8:8 + tt, :] = xa
    xc = cb_ref[...] + cw_ref[CONV_WIDTH - 1:CONV_WIDTH, :] * xa
    for i in range(CONV_WIDTH - 1):
        xc = xc + cw_ref[i:i + 1, :] * xbuf[5 + i:5 + i + tt, :]
    a, mult, i_g = _lru_gates(xc, wa_ref[...], ba_ref[...], wx_ref[...], bx_ref[...], lam_ref[...])
    pos = _iota((tt, 1), 0) + t * tt
    mult = jnp.where(pos == 0, 1.0, mult)
    b = mult * i_g * xc
    sa[pad:pad + tt, :] = a
    sb[pad:pad + tt, :] = b
    sb[pad:pad + 1, :] = b[0:1, :] + a[0:1, :] * hc[...]

    d = 1
    while d < tt:
        a_cur = sa[pad:pad + tt, :]
        b_cur = sb[pad:pad + tt, :]
        a_sh = sa[pad - d:pad - d + tt, :]
        b_sh = sb[pad - d:pad - d + tt, :]
        sb[pad:pad + tt, :] = a_cur * b_sh + b_cur
        if 2 * d < tt:
            sa[pad:pad + tt, :] = a_cur * a_sh
        d *= 2

    h = sb[pad:pad + tt, :]
    y_ref[...] = _gelu_tanh(ga_ref[...]) * h
    hc[...] = h[tt - 1:tt, :]
    xbuf[0:8, :] = xa[tt - 8:tt, :]
    conv_ref[...] = xa[tt - (CONV_WIDTH - 1):tt, :]
    h_ref[...] = h[tt - 1:tt, :]


def _rglru_prompt(u3, cw, cb, wa, ba, wx, bx, lam, tt=512):
    bsz, t, _ = u3.shape
    full = lambda shape: pl.BlockSpec(shape, lambda b, i: (0,) * len(shape))
    return pl.pallas_call(
        _rglru_prompt_kernel,
        grid=(bsz, t // tt),
        in_specs=[pl.BlockSpec((None, tt, MIX_W), lambda b, i: (b, i, COL_XA)),
                  pl.BlockSpec((None, tt, MIX_W), lambda b, i: (b, i, COL_GA)),
                  full((CONV_WIDTH, MIX_W)), full((1, MIX_W)), full((MIX_W, MIX_W)), full((1, MIX_W)),
                  full((MIX_W, MIX_W)), full((1, MIX_W)), full((1, MIX_W))],
        out_specs=[pl.BlockSpec((None, tt, MIX_W), lambda b, i: (b, i, 0)),
                   pl.BlockSpec((None, CONV_WIDTH - 1, MIX_W), lambda b, i: (b, 0, 0)),
                   pl.BlockSpec((None, 1, MIX_W), lambda b, i: (b, 0, 0))],
        out_shape=[jax.ShapeDtypeStruct((bsz, t, MIX_W), F32),
                   jax.ShapeDtypeStruct((bsz, CONV_WIDTH - 1, MIX_W), F32),
                   jax.ShapeDtypeStruct((bsz, 1, MIX_W), F32)],
        scratch_shapes=[pltpu.VMEM((tt + 8, MIX_W), F32),
                        pltpu.VMEM((tt + tt // 2, MIX_W), F32),
                        pltpu.VMEM((tt + tt // 2, MIX_W), F32),
                        pltpu.VMEM((1, MIX_W), F32)],
        compiler_params=_cparams("parallel", "arbitrary"),
    )(u3, u3, cw, cb, wa, ba, wx, bx, lam)


def _rglru_decode_kernel(xa_ref, ga_ref, conv_ref, h0_ref, cw_ref, cb_ref, wa_ref, ba_ref, wx_ref, bx_ref,
                         lam_ref, y_ref, convn_ref, h_ref):
    xa = xa_ref[...]
    w = MIX_W
    xc = cb_ref[...] + cw_ref[CONV_WIDTH - 1:CONV_WIDTH, :] * xa
    for i in range(CONV_WIDTH - 1):
        xc = xc + cw_ref[i:i + 1, :] * conv_ref[:, i * w:(i + 1) * w]
    a, mult, i_g = _lru_gates(xc, wa_ref[...], ba_ref[...], wx_ref[...], bx_ref[...], lam_ref[...])
    h = a * h0_ref[...] + mult * i_g * xc
    y_ref[...] = _gelu_tanh(ga_ref[...]) * h
    h_ref[...] = h
    convn_ref[:, 0:(CONV_WIDTH - 2) * w] = conv_ref[:, w:(CONV_WIDTH - 1) * w]
    convn_ref[:, (CONV_WIDTH - 2) * w:(CONV_WIDTH - 1) * w] = xa


def _rglru_decode(u, conv, h0, cw, cb, wa, ba, wx, bx, lam):
    n = u.shape[0]
    cwid = (CONV_WIDTH - 1) * MIX_W
    full = lambda shape: pl.BlockSpec(shape, lambda i: (0,) * len(shape))
    return pl.pallas_call(
        _rglru_decode_kernel,
        grid=(1,),
        in_specs=[pl.BlockSpec((n, MIX_W), lambda i: (0, COL_XA)),
                  pl.BlockSpec((n, MIX_W), lambda i: (0, COL_GA)),
                  full((n, cwid)), full((n, MIX_W)),
                  full((CONV_WIDTH, MIX_W)), full((1, MIX_W)), full((MIX_W, MIX_W)), full((1, MIX_W)),
                  full((MIX_W, MIX_W)), full((1, MIX_W)), full((1, MIX_W))],
        out_specs=[full((n, MIX_W)), full((n, cwid)), full((n, MIX_W))],
        out_shape=[jax.ShapeDtypeStruct((n, MIX_W), F32),
                   jax.ShapeDtypeStruct((n, cwid), F32),
                   jax.ShapeDtypeStruct((n, MIX_W), F32)],
        compiler_params=_cparams("arbitrary"),
    )(u, u, conv, h0, cw, cb, wa, ba, wx, bx, lam)


def _alibi_slope(h):
    return 2.0 ** (-8.0 * (h + 1) / N_HEADS)


def _diff_lambda(lam_ref, cst_ref):
    lv = lam_ref[...]
    s1 = jnp.sum(lv[0:1, :] * lv[1:2, :], axis=1, keepdims=True)
    s2 = jnp.sum(lv[2:3, :] * lv[3:4, :], axis=1, keepdims=True)
    lam_init = cst_ref[0:1, 0:1]
    return jnp.exp(s1) - jnp.exp(s2) + lam_init, lam_init


def _head_slab(h):
    return slice((h // 2) * LANES, (h // 2 + 1) * LANES), (h % 2) * HEAD_DIM


def _diff_q_rows(q, h):
    lane = _iota((1, LANES), 1)
    slab, lo = _head_slab(h)
    qs = q[:, slab] * (DIFF_QK_DIM ** -0.5 * LOG2E)
    q1 = jnp.where((lane >= lo) & (lane < lo + DIFF_QK_DIM), qs, 0.0)
    q2 = jnp.where((lane >= lo + DIFF_QK_DIM) & (lane < lo + HEAD_DIM), qs, 0.0)
    return jnp.concatenate([q1, q2], axis=0).astype(BF16)


def _diff_sum_lane(h):
    return HEAD_DIM if h % 2 == 0 else 0


def _diff_prompt_kernel(q_ref, k_ref, v_ref, lam_ref, cst_ref, g_ref, y_ref, kb_s, va_s, qs_s, m_s, acc_s):
    i = pl.program_id(1)
    tq = q_ref.shape[0]
    tk = tq
    t_all = k_ref.shape[0]
    rows = 2 * tq
    nl = tk // LANES

    @pl.when(i == 0)
    def _():
        lane = _iota((1, LANES), 1)

        def prep(c, carry):
            r0 = pl.multiple_of(c * tk, tk)
            kb_s[pl.ds(r0, tk), :] = k_ref[pl.ds(r0, tk), :].astype(BF16)
            v = v_ref[pl.ds(r0, tk), :]
            for h in range(N_HEADS):
                slab = v[:, (h // 2) * LANES:(h // 2 + 1) * LANES]
                va_s[h, pl.ds(r0, tk), :] = jnp.where(lane == _diff_sum_lane(h), 1.0, slab).astype(BF16)
            return carry

        lax.fori_loop(0, t_all // tk, prep, 0)

    q = q_ref[...]
    for h in range(N_HEADS):
        qs_s[h] = _diff_q_rows(q, h)
    m_s[...] = jnp.full(m_s.shape, NEG_BIG, F32)
    acc_s[...] = jnp.zeros_like(acc_s)
    row_in_tile = _iota((rows, LANES), 0) % tq
    col = _iota((rows, LANES), 1)

    def block(j, masked):
        k0 = pl.multiple_of(j * tk, tk)
        kpos = (_iota((1, tk), 1) + ((j - i) * tk - (tq - 1))).astype(F32)
        for h in range(N_HEADS):
            slab, _ = _head_slab(h)
            s = _dot_nt(qs_s[h], kb_s[pl.ds(k0, tk), slab])
            bias = kpos * (_alibi_slope(h) * LOG2E)
            cols = []
            for c in range(nl):
                sc = s[:, c * LANES:(c + 1) * LANES] + bias[:, c * LANES:(c + 1) * LANES]
                if masked:
                    sc = jnp.where(col + c * LANES <= row_in_tile, sc, NEG_BIG)
                cols.append(sc)
            mx = cols[0]
            for sc in cols[1:]:
                mx = jnp.maximum(mx, sc)
            m_prev = m_s[h]
            m_new = jnp.maximum(m_prev, jnp.max(mx, axis=1, keepdims=True))
            p = jnp.concatenate([jnp.exp2(sc - m_new) for sc in cols], axis=1).astype(BF16)
            acc_s[h] = jnp.exp2(m_prev - m_new) * acc_s[h] + _dot(p, va_s[h, pl.ds(k0, tk), :])
            m_s[h] = m_new

    def body(j, carry):
        block(j, False)
        return carry

    lax.fori_loop(0, i, body, 0)
    block(i, True)
    lam, lam_init = _diff_lambda(lam_ref, cst_ref)
    outs = []
    for h in range(N_HEADS):
        acc = acc_s[h]
        lo = (h % 2) * HEAD_DIM
        sl = _diff_sum_lane(h)
        o = acc[:, lo:lo + HEAD_DIM] / acc[:, sl:sl + 1]
        o = o[0:tq, :] - lam * o[tq:rows, :]
        inv = lax.rsqrt(jnp.mean(o * o, axis=1, keepdims=True) + RMS_EPS)
        outs.append(o * inv * g_ref[...] * (1.0 - lam_init))
    y_ref[...] = jnp.concatenate(outs, axis=1)


def _diff_prompt(u3, lam_vecs, consts, norm_g, tq=512):
    bsz, t, _ = u3.shape
    full = lambda shape: pl.BlockSpec(shape, lambda b, i: (0,) * len(shape))
    return pl.pallas_call(
        _diff_prompt_kernel,
        grid=(bsz, t // tq),
        in_specs=[pl.BlockSpec((None, tq, MIX_W), lambda b, i: (b, i, COL_DQ)),
                  pl.BlockSpec((None, t, MIX_W), lambda b, i: (b, 0, COL_DK)),
                  pl.BlockSpec((None, t, MIX_W), lambda b, i: (b, 0, COL_DV)),
                  full(lam_vecs.shape), full(consts.shape), full(norm_g.shape)],
        out_specs=pl.BlockSpec((None, tq, MIX_W), lambda b, i: (b, i, 0)),
        out_shape=jax.ShapeDtypeStruct((bsz, t, MIX_W), F32),
        scratch_shapes=[pltpu.VMEM((t, MIX_W), BF16), pltpu.VMEM((N_HEADS, t, LANES), BF16),
                        pltpu.VMEM((N_HEADS, 2 * tq, LANES), BF16),
                        pltpu.VMEM((N_HEADS, 2 * tq, LANES), F32), pltpu.VMEM((N_HEADS, 2 * tq, LANES), F32)],
        compiler_params=_cparams("parallel", "arbitrary"),
    )(u3, u3, u3, lam_vecs, consts, norm_g)


def _neg_log2_keep(z2):
    e = jnp.exp2(jnp.minimum(z2, -z2))
    return jnp.maximum(z2, 0.0) + jnp.log2(1.0 + e)


def _strict_upper(n):
    return jnp.where(_iota((n, n), 0) > _iota((n, n), 1), 1.0, 0.0).astype(BF16)


def _sb_prompt_kernel(q_ref, k_ref, v_ref, y_ref, kb_s, vb_s, qs_s, r_s, acc_s, *, tk):
    i = pl.program_id(1)
    tq = q_ref.shape[0]
    t_all = k_ref.shape[0]
    nl = tk // LANES
    per_tile = tq // tk

    @pl.when(i == 0)
    def _():
        def prep(c, carry):
            r0 = pl.multiple_of(c * tq, tq)
            kb_s[pl.ds(r0, tq), :] = k_ref[pl.ds(r0, tq), :].astype(BF16)
            vb_s[pl.ds(r0, tq), :] = v_ref[pl.ds(r0, tq), :].astype(BF16)
            return carry

        lax.fori_loop(0, t_all // tq, prep, 0)

    q = q_ref[...]
    lane = _iota((1, LANES), 1)
    for h in range(N_HEADS):
        slab, lo = _head_slab(h)
        qs_s[h] = jnp.where((lane >= lo) & (lane < lo + HEAD_DIM), q[:, slab] * (HEAD_DIM ** -0.5 * LOG2E),
                            0.0).astype(BF16)
    r_s[...] = jnp.zeros_like(r_s)
    acc_s[...] = jnp.zeros_like(acc_s)
    row = _iota((tq, LANES), 0)
    col = _iota((tq, LANES), 1)
    upper_incl = jnp.where(_iota((tk, tk), 0) >= _iota((tk, tk), 1), 1.0, 0.0).astype(BF16)

    def block(j, key_off):
        k0 = pl.multiple_of(j * tk, tk)
        earlier = None
        if key_off is not None:
            earlier = [col + (c * LANES + key_off) < row for c in range(nl)]
        heads = range(N_HEADS)
        z2s = [_dot_nt(qs_s[h], kb_s[pl.ds(k0, tk), _head_slab(h)[0]]) for h in heads]
        nks = [_neg_log2_keep(z2) for z2 in z2s]
        if earlier is not None:
            nks = [jnp.concatenate([jnp.where(earlier[c], nk[:, c * LANES:(c + 1) * LANES], 0.0)
                                    for c in range(nl)], axis=1) for nk in nks]
        incl = [_split_dot(nk, upper_incl) for nk in nks]
        for h in heads:
            rr = r_s[h]
            ws = []
            for c in range(nl):
                sl = slice(c * LANES, (c + 1) * LANES)
                wc = jnp.exp2(jnp.minimum(z2s[h][:, sl] - incl[h][:, sl] - rr, 0.0))
                if earlier is not None:
                    wc = jnp.where(earlier[c], wc, 0.0)
                ws.append(wc)
            w = jnp.concatenate(ws, axis=1).astype(BF16)
            acc_s[h] = acc_s[h] + _dot(w, vb_s[pl.ds(k0, tk), _head_slab(h)[0]])
            r_s[h] = rr + jnp.sum(nks[h], axis=1, keepdims=True)

    for d in range(per_tile - 1, -1, -1):
        block(i * per_tile + d, d * tk)

    def body(jj, carry):
        block(i * per_tile - 1 - jj, None)
        return carry

    lax.fori_loop(0, i * per_tile, body, 0)
    y_ref[...] = jnp.concatenate([acc_s[h][:, _head_slab(h)[1]:_head_slab(h)[1] + HEAD_DIM]
                                  for h in range(N_HEADS)], axis=1)


def _sb_prompt(u3, tq=512, tk=256):
    bsz, t, _ = u3.shape
    return pl.pallas_call(
        functools.partial(_sb_prompt_kernel, tk=tk),
        grid=(bsz, t // tq),
        in_specs=[pl.BlockSpec((None, tq, MIX_W), lambda b, i: (b, i, COL_SQ)),
                  pl.BlockSpec((None, t, MIX_W), lambda b, i: (b, 0, COL_SK)),
                  pl.BlockSpec((None, t, MIX_W), lambda b, i: (b, 0, COL_SV))],
        out_specs=pl.BlockSpec((None, tq, MIX_W), lambda b, i: (b, i, 0)),
        out_shape=jax.ShapeDtypeStruct((bsz, t, MIX_W), F32),
        scratch_shapes=[pltpu.VMEM((t, MIX_W), BF16), pltpu.VMEM((t, MIX_W), BF16),
                        pltpu.VMEM((N_HEADS, tq, LANES), BF16),
                        pltpu.VMEM((N_HEADS, tq, LANES), F32), pltpu.VMEM((N_HEADS, tq, LANES), F32)],
        compiler_params=_cparams("parallel", "arbitrary"),
    )(u3, u3, u3)


def _decode_attn_kernel(pt_ref, *refs, n_pages):
    del pt_ref
    np_ = n_pages
    dk_refs = refs[0:np_]
    dv_refs = refs[np_:2 * np_]
    sk_refs = refs[2 * np_:3 * np_]
    sv_refs = refs[3 * np_:4 * np_]
    (dq_ref, dkn_ref, dvn_ref, sq_ref, lam_ref, cst_ref, g_ref, yb_ref, yc_ref) = refs[4 * np_:]
    past = np_ * PAGE_SIZE
    w = MIX_W
    lane = _iota((1, w), 1)
    row8 = _iota((2 * N_HEADS, 1), 0)
    cat = lambda page_refs: jnp.concatenate([r[...].astype(BF16) for r in page_refs], axis=1)

    q = dq_ref[...] * (DIFF_QK_DIM ** -0.5 * LOG2E)
    seg = lane // DIFF_QK_DIM
    qrows = jnp.where(seg == row8, q, 0.0)
    slope2 = jnp.exp((row8 // 2 + 1).astype(F32) * (-8.0 * math.log(2.0) / N_HEADS)) * LOG2E
    kpos = _iota((1, past), 1).astype(F32)
    s = _dot(qrows.astype(BF16), cat(dk_refs)) - slope2 * (float(past) - kpos)
    s_new = jnp.sum(qrows * dkn_ref[...], axis=1, keepdims=True)
    m = jnp.maximum(s_new, jnp.max(s, axis=1, keepdims=True))
    p_new = jnp.exp2(s_new - m)
    pr = jnp.exp2(s - m)
    l = p_new + jnp.sum(pr, axis=1, keepdims=True)
    acc = p_new * dvn_ref[...] + _dot_nt(pr.astype(BF16), cat(dv_refs))
    o = acc / l
    lam, lam_init = _diff_lambda(lam_ref, cst_ref)
    coef = jnp.where(row8 % 2 == 0, 1.0, -lam)
    head_of_lane = lane // HEAD_DIM
    o = jnp.where(head_of_lane == row8 // 2, o * coef, 0.0)
    o = jnp.sum(o, axis=0, keepdims=True)
    ms = _split_dot(o * o, _head_ones(w).astype(BF16)) * (1.0 / HEAD_DIM)
    yb_ref[...] = o * lax.rsqrt(ms + RMS_EPS) * g_ref[...] * (1.0 - lam_init)

    rowh = _iota((N_HEADS, 1), 0)
    qsb = jnp.where(head_of_lane == rowh, sq_ref[...] * (HEAD_DIM ** -0.5 * LOG2E), 0.0).astype(BF16)
    z2 = _dot(qsb, cat(sk_refs))
    nk = _neg_log2_keep(z2)
    chunk = 2 * PAGE_SIZE
    upper = _strict_upper(chunk)
    run = jnp.zeros((N_HEADS, 1), F32)
    later = [None] * (past // chunk)
    for c in range(past // chunk - 1, -1, -1):
        nk_c = nk[:, c * chunk:(c + 1) * chunk]
        later[c] = _split_dot(nk_c, upper) + run
        run = run + jnp.sum(nk_c, axis=1, keepdims=True)
    wgt = jnp.exp2(jnp.minimum(z2 - nk - jnp.concatenate(later, axis=1), 0.0))
    acc = _dot_nt(wgt.astype(BF16), cat(sv_refs))
    yc_ref[...] = jnp.sum(jnp.where(head_of_lane == rowh, acc, 0.0), axis=0, keepdims=True)


def _decode_attn(layer, page_table, caches, u3, lam_vecs, consts, norm_g_tiled):
    n, n_pages = page_table.shape

    def page_spec(p):
        return pl.BlockSpec((None, None, MIX_W, PAGE_SIZE), lambda b, pt: (layer, pt[b, p], 0, 0))

    def col_spec(c):
        return pl.BlockSpec((None, 1, MIX_W), lambda b, pt: (b, 0, c))

    full = lambda shape: pl.BlockSpec(shape, lambda b, pt: (0,) * len(shape))
    in_specs = [page_spec(p) for _ in range(4) for p in range(n_pages)]
    in_specs += [col_spec(COL_DQ), col_spec(COL_DK), col_spec(COL_DV), col_spec(COL_SQ),
                 full(lam_vecs.shape), full(consts.shape), full(norm_g_tiled.shape)]
    operands = [c for c in caches for _ in range(n_pages)] + [u3, u3, u3, u3, lam_vecs, consts, norm_g_tiled]
    out_spec = pl.BlockSpec((None, 1, MIX_W), lambda b, pt: (b, 0, 0))
    return pl.pallas_call(
        functools.partial(_decode_attn_kernel, n_pages=n_pages),
        grid_spec=pltpu.PrefetchScalarGridSpec(
            num_scalar_prefetch=1, grid=(n,), in_specs=in_specs, out_specs=[out_spec, out_spec]),
        out_shape=[jax.ShapeDtypeStruct((n, 1, MIX_W), F32), jax.ShapeDtypeStruct((n, 1, MIX_W), F32)],
        compiler_params=_cparams("parallel"),
    )(page_table, *operands)


def _hgrn_gates(hq, hf, lb):
    q = _silu(hq)
    e = jnp.exp(-jnp.abs(hf))
    inv = 1.0 / (1.0 + e)
    pos = hf >= 0.0
    sig = jnp.where(pos, inv, e * inv)
    sig_n = jnp.where(pos, e * inv, inv)
    f = lb + (1.0 - lb) * sig
    return q, f, (1.0 - lb) * sig_n


def _hgrn_out(o, hg, g, ones_bf16):
    ms = _split_dot(o * o, ones_bf16) * (1.0 / HEAD_DIM)
    return o * lax.rsqrt(ms + RMS_EPS) * g * _silu(hg)


def _hgrn_prompt_kernel(hq_ref, hf_ref, hi_ref, hg_ref, lb_ref, g_ref, y_ref, st_ref, st_s):
    t = pl.program_id(1)
    tt = hq_ref.shape[0]
    c = HG_CHUNK
    w = MIX_W

    @pl.when(t == 0)
    def _():
        st_s[...] = jnp.zeros_like(st_s)

    lb = lb_ref[...]
    ones_f = _head_ones(w)
    ones_b = ones_f.astype(BF16)
    tril = jnp.where(_iota((c, c), 1) <= _iota((c, c), 0), 1.0, 0.0).astype(BF16)
    row = _iota((c, 1), 0)
    for ci in range(tt // c):
        sl = slice(ci * c, (ci + 1) * c)
        q, f, k = _hgrn_gates(hq_ref[sl, :], hf_ref[sl, :], lb)
        v = hi_ref[sl, :]
        cum = _split_dot_left(tril, jnp.log(jnp.maximum(f, F_FLOOR)))
        st = st_s[...]
        o = _dot_nt((q * jnp.exp(cum)).astype(BF16), st.astype(BF16))
        for j in range(c // HG_SUB - 1):
            s0, s1 = j * HG_SUB, (j + 1) * HG_SUB
            ref_row = cum[s1 - 1:s1, :]
            qj = q * jnp.exp(jnp.minimum(cum - ref_row, 0.0))
            kj = k[s0:s1, :] * jnp.exp(ref_row - cum[s0:s1, :])
            mt = _dot_tn(v[s0:s1, :].astype(BF16), kj.astype(BF16)) * ones_f
            oj = _dot_nt(qj.astype(BF16), mt.astype(BF16))
            o = o + jnp.where(row >= s1, oj, 0.0)
        for lag in range(HG_SUB):
            if lag == 0:
                ks, cs, vs = k, cum, v
            else:
                ks = pltpu.roll(k, lag, 0)
                cs = pltpu.roll(cum, lag, 0)
                vs = pltpu.roll(v, lag, 0)
            term = q * ks * jnp.exp(jnp.minimum(cum - cs, 0.0))
            ssum = _dot(term.astype(BF16), ones_b)
            o = o + jnp.where(row % HG_SUB >= lag, ssum * vs, 0.0)
        last = cum[c - 1:c, :]
        kc = k * jnp.exp(last - cum)
        st_s[...] = st * jnp.exp(last) + _dot_tn(v.astype(BF16), kc.astype(BF16)) * ones_f
        y_ref[sl, :] = _hgrn_out(o, hg_ref[sl, :], g_ref[...], ones_b)
    st_ref[...] = st_s[...]


def _split_dot_left(w_bf16, x):
    hi = x.astype(BF16)
    lo = (x - hi.astype(F32)).astype(BF16)
    return _dot(w_bf16, hi) + _dot(w_bf16, lo)


def _hgrn_prompt(u3, lb, g_tiled, tt=256):
    bsz, t, _ = u3.shape
    full = lambda shape: pl.BlockSpec(shape, lambda b, i: (0,) * len(shape))
    col = lambda c: pl.BlockSpec((None, tt, MIX_W), lambda b, i: (b, i, c))
    return pl.pallas_call(
        _hgrn_prompt_kernel,
        grid=(bsz, t // tt),
        in_specs=[col(COL_HQ), col(COL_HF), col(COL_HI), col(COL_HG), full((1, MIX_W)), full((1, MIX_W))],
        out_specs=[pl.BlockSpec((None, tt, MIX_W), lambda b, i: (b, i, 0)),
                   pl.BlockSpec((None, MIX_W, MIX_W), lambda b, i: (b, 0, 0))],
        out_shape=[jax.ShapeDtypeStruct((bsz, t, MIX_W), F32),
                   jax.ShapeDtypeStruct((bsz, MIX_W, MIX_W), F32)],
        scratch_shapes=[pltpu.VMEM((MIX_W, MIX_W), F32)],
        compiler_params=_cparams("parallel", "arbitrary"),
    )(u3, u3, u3, u3, lb, g_tiled)


def _hgrn_decode_kernel(s0_ref, q_ref, f_ref, v_ref, lb_ref, sn_ref, o_ref):
    bb = s0_ref.shape[0]
    q, f, k = _hgrn_gates(q_ref[...], f_ref[...], lb_ref[...])
    sn = f * s0_ref[...] + k * v_ref[...]
    sn_ref[...] = sn
    o_ref[...] = jnp.sum((q * sn).reshape(bb, N_HEADS, HEAD_DIM, HEAD_DIM), axis=2)


def _hgrn_decode(s0, q_e, f_e, v_e, lb_e, bb=8):
    n = s0.shape[0]
    blk = pl.BlockSpec((bb, MIX_W, HEAD_DIM), lambda i: (i, 0, 0))
    return pl.pallas_call(
        _hgrn_decode_kernel,
        grid=(n // bb,),
        in_specs=[blk, blk, blk, blk, pl.BlockSpec((1, MIX_W, HEAD_DIM), lambda i: (0, 0, 0))],
        out_specs=[blk, pl.BlockSpec((bb, N_HEADS, HEAD_DIM), lambda i: (i, 0, 0))],
        out_shape=[jax.ShapeDtypeStruct((n, MIX_W, HEAD_DIM), F32),
                   jax.ShapeDtypeStruct((n, N_HEADS, HEAD_DIM), F32)],
        compiler_params=_cparams("parallel"),
    )(s0, q_e, f_e, v_e, lb_e)


def _hgrn_out_kernel(o_ref, hg_ref, g_ref, y_ref):
    y_ref[...] = _hgrn_out(o_ref[...], hg_ref[...], g_ref[...], _head_ones(MIX_W).astype(BF16))


def _hgrn_decode_out(o, u, g_tiled):
    n = o.shape[0]
    full = lambda shape: pl.BlockSpec(shape, lambda i: (0,) * len(shape))
    return pl.pallas_call(
        _hgrn_out_kernel,
        grid=(1,),
        in_specs=[full((n, MIX_W)), pl.BlockSpec((n, MIX_W), lambda i: (0, COL_HG)), full((1, MIX_W))],
        out_specs=full((n, MIX_W)),
        out_shape=jax.ShapeDtypeStruct((n, MIX_W), F32),
        compiler_params=_cparams("arbitrary"),
    )(o, u, g_tiled)


def _merge_kernel(ya_ref, yb_ref, yc_ref, yd_ref, g0_ref, g1_ref, g2_ref, g3_ref, x_ref, wb_ref, wo_ref,
                  lg_ref, lbias_ref, o_ref, *, alpha):
    merged = None
    for y_ref, gl_ref, n in ((ya_ref, g0_ref, 0), (yb_ref, g1_ref, 1), (yc_ref, g2_ref, 2), (yd_ref, g3_ref, 3)):
        term = _sigmoid(gl_ref[...]) * _dot(y_ref[...].astype(BF16), wb_ref[n])
        merged = term if merged is None else merged + term
    out = _dot(merged.astype(BF16), wo_ref[...])
    o_ref[...] = _layer_norm(alpha * x_ref[...] + out, lg_ref[...], lbias_ref[...])


def _merge(ya, yb, yc, yd, u, x, wb, wo, ln_g, ln_b, alpha):
    n, d = x.shape
    tm = min(256, n)
    gate0 = N_MIX_COLS * MIX_W // d
    row = lambda width, c=0: pl.BlockSpec((tm, width), lambda i, c=c: (i, c))
    full = lambda shape: pl.BlockSpec(shape, lambda i: (0,) * len(shape))
    return pl.pallas_call(
        functools.partial(_merge_kernel, alpha=alpha),
        grid=(n // tm,),
        in_specs=[row(MIX_W), row(MIX_W), row(MIX_W), row(MIX_W),
                  row(d, gate0), row(d, gate0 + 1), row(d, gate0 + 2), row(d, gate0 + 3),
                  row(d), full(wb.shape), full(wo.shape), full((1, d)), full((1, d))],
        out_specs=row(d),
        out_shape=jax.ShapeDtypeStruct((n, d), F32),
        compiler_params=_cparams("parallel"),
    )(ya, yb, yc, yd, u, u, u, u, x, wb, wo, ln_g, ln_b)


def _router_weights(x, wr_hi, wr_lo, br):
    xh = x.astype(BF16)
    xl = (x - xh.astype(F32)).astype(BF16)
    logits = _dot(xh, wr_hi) + _dot(xh, wr_lo) + _dot(xl, wr_hi) + br
    lane = _iota((1, ROUTER_LANES), 1)
    big = jnp.int32(ROUTER_LANES)
    is_g = lane < N_GROUPS
    gl = jnp.where(is_g, logits, NEG_BIG)
    gmax = jnp.max(gl, axis=1, keepdims=True)
    g_idx = jnp.min(jnp.where(is_g & (gl == gmax), lane, big), axis=1, keepdims=True)
    g_w = 1.0 / jnp.sum(jnp.where(is_g, jnp.exp(gl - gmax), 0.0), axis=1, keepdims=True)
    in_grp = (lane >= ROUTER_E0) & (lane < ROUTER_E0 + N_EXPERTS) & \
             ((lane - ROUTER_E0) // EXPERTS_PER_GROUP == g_idx)
    el = jnp.where(in_grp, logits, NEG_BIG)
    v1 = jnp.max(el, axis=1, keepdims=True)
    i1 = jnp.min(jnp.where(in_grp & (el == v1), lane, big), axis=1, keepdims=True)
    el2 = jnp.where(lane == i1, NEG_BIG, el)
    v2 = jnp.max(el2, axis=1, keepdims=True)
    i2 = jnp.min(jnp.where(in_grp & (lane != i1) & (el2 == v2), lane, big), axis=1, keepdims=True)
    e2 = jnp.exp(v2 - v1)
    w1 = g_w / (1.0 + e2)
    w2 = g_w * e2 / (1.0 + e2)
    return jnp.where(lane == i1, w1, 0.0) + jnp.where(lane == i2, w2, 0.0)


def _moe_kernel(x_ref, wrh_ref, wrl_ref, br_ref, w1_ref, w3_ref, w2_ref, lg_ref, lbias_ref, o_ref,
                comb_s, acc_s, *, alpha):
    e = pl.program_id(1)
    x = x_ref[...]

    @pl.when(e == 0)
    def _():
        comb_s[...] = _router_weights(x, wrh_ref[...], wrl_ref[...], br_ref[...])
        acc_s[...] = jnp.zeros_like(acc_s)

    lane = _iota((1, ROUTER_LANES), 1)
    c_e = jnp.sum(jnp.where(lane == ROUTER_E0 + e, comb_s[...], 0.0), axis=1, keepdims=True)
    xb = x.astype(BF16)
    hid = _silu(_dot(xb, w1_ref[...])) * _dot(xb, w3_ref[...]) * c_e
    acc_s[...] += _dot(hid.astype(BF16), w2_ref[...])

    @pl.when(e == pl.num_programs(1) - 1)
    def _():
        o_ref[...] = _layer_norm(alpha * x + acc_s[...], lg_ref[...], lbias_ref[...])


def _moe(x, wr_hi, wr_lo, br, w1, w3, w2, ln_g, ln_b, alpha):
    n, d = x.shape
    ne, _, ff = w1.shape
    tm = min(1024, n)
    full = lambda shape: pl.BlockSpec(shape, lambda i, e: (0,) * len(shape))
    return pl.pallas_call(
        functools.partial(_moe_kernel, alpha=alpha),
        grid=(n // tm, ne),
        in_specs=[pl.BlockSpec((tm, d), lambda i, e: (i, 0)),
                  full(wr_hi.shape), full(wr_lo.shape), full(br.shape),
                  pl.BlockSpec((None, d, ff), lambda i, e: (e, 0, 0)),
                  pl.BlockSpec((None, d, ff), lambda i, e: (e, 0, 0)),
                  pl.BlockSpec((None, ff, d), lambda i, e: (e, 0, 0)),
                  full((1, d)), full((1, d))],
        out_specs=pl.BlockSpec((tm, d), lambda i, e: (i, 0)),
        out_shape=jax.ShapeDtypeStruct((n, d), F32),
        scratch_shapes=[pltpu.VMEM((tm, ROUTER_LANES), F32), pltpu.VMEM((tm, d), F32)],
        compiler_params=_cparams("parallel", "arbitrary"),
    )(x, wr_hi, wr_lo, br, w1, w3, w2, ln_g, ln_b)


def _block_diag(w):
    nb, n, _ = w.shape
    eye = jnp.eye(nb, dtype=w.dtype)
    return (eye[:, None, :, None] * w[:, :, None, :]).reshape(nb * n, nb * n)


def _router_matrix(wg, bg, we, be):
    d = wg.shape[0]
    wr = jnp.zeros((d, ROUTER_LANES), F32)
    wr = wr.at[:, 0:N_GROUPS].set(wg).at[:, ROUTER_E0:ROUTER_E0 + N_EXPERTS].set(we)
    br = jnp.zeros((1, ROUTER_LANES), F32)
    br = br.at[0, 0:N_GROUPS].set(bg).at[0, ROUTER_E0:ROUTER_E0 + N_EXPERTS].set(be)
    hi = wr.astype(BF16)
    lo = (wr - hi.astype(F32)).astype(BF16)
    return hi, lo, br


def kernel(x_prompt, x_sample, cache_diff_k, cache_diff_v, cache_sb_k, cache_sb_v, page_table, state_conv, state_lru, state_hgrn, w_in, conv_w, conv_b, lru_wa, lru_ba, lru_wx, lru_bx, lru_lambda, diff_lam_q1, diff_lam_k1, diff_lam_q2, diff_lam_k2, diff_norm_g, hgrn_lb_raw, hgrn_norm_g, w_branch, w_out, ln1_g, ln1_b, router_group_w, router_group_b, router_expert_w, router_expert_b, exp_w1, exp_w3, exp_w2, ln2_g, ln2_b):
    depth = w_in.shape[0]
    bsz, seq, d = x_prompt.shape
    nd = x_sample.shape[0]
    n_pool = cache_diff_k.shape[1]
    alpha = (2 * depth) ** 0.25
    row = lambda a: a.reshape(1, -1)

    hg_lb = _hgrn_lower_bounds(hgrn_lb_raw)
    caches = [jnp.transpose(c, (0, 1, 3, 4, 2)).reshape(depth, n_pool, MIX_W, PAGE_SIZE)
              for c in (cache_diff_k, cache_diff_v, cache_sb_k, cache_sb_v)]

    xp = x_prompt.reshape(bsz * seq, d)
    xs = x_sample.reshape(nd, d)
    outs_p = [[] for _ in range(7)]
    outs_s = [[] for _ in range(7)]
    for l in range(depth):
        w_in_b = w_in[l].astype(BF16)
        wa = _block_diag(lru_wa[l]).astype(BF16)
        wx = _block_diag(lru_wx[l]).astype(BF16)
        lru = (conv_w[l], row(conv_b[l]), wa, row(lru_ba[l]), wx, row(lru_bx[l]), row(lru_lambda[l]))
        lam_vecs = jnp.stack([diff_lam_q1[l], diff_lam_k1[l], diff_lam_q2[l], diff_lam_k2[l]])
        consts = jnp.zeros((1, 128), F32).at[0, 0].set(0.8 - 0.6 * math.exp(-0.3 * l))
        dn_g = row(diff_norm_g[l])
        dn_g_t = jnp.tile(dn_g, (1, N_HEADS))
        hg_g_t = jnp.tile(row(hgrn_norm_g[l]), (1, N_HEADS))
        lb = row(hg_lb[l])
        wb = w_branch[l].astype(BF16)
        wo = w_out[l].astype(BF16)
        wr_hi, wr_lo, br = _router_matrix(router_group_w[l], router_group_b[l], router_expert_w[l], router_expert_b[l])
        w1, w3, w2 = exp_w1[l].astype(BF16), exp_w3[l].astype(BF16), exp_w2[l].astype(BF16)
        merge_w = (wb, wo, row(ln1_g[l]), row(ln1_b[l]))
        moe_w = (wr_hi, wr_lo, br, w1, w3, w2, row(ln2_g[l]), row(ln2_b[l]))

        u = _in_proj(xp, w_in_b)
        u3 = u.reshape(bsz, seq, -1)
        ya, conv_p, lru_p = _rglru_prompt(u3, *lru)
        yb = _diff_prompt(u3, lam_vecs, consts, dn_g)
        yc = _sb_prompt(u3)
        yd, st_t = _hgrn_prompt(u3, lb, hg_g_t)
        flat = lambda a: a.reshape(bsz * seq, MIX_W)
        x1 = _merge(flat(ya), flat(yb), flat(yc), flat(yd), u, xp, *merge_w, alpha)
        xp = _moe(x1, *moe_w, alpha)
        heads = lambda c: u[:, c * MIX_W:(c + 1) * MIX_W].reshape(bsz, seq, N_HEADS, HEAD_DIM)
        st = jnp.stack([st_t[:, h * HEAD_DIM:(h + 1) * HEAD_DIM, h * HEAD_DIM:(h + 1) * HEAD_DIM]
                        for h in range(N_HEADS)], axis=1).swapaxes(-1, -2)
        for lst, val in zip(outs_p, (heads(COL_DK), heads(COL_DV), heads(COL_SK), heads(COL_SV), conv_p,
                                     lru_p.reshape(bsz, MIX_W), st)):
            lst.append(val)

        us = _in_proj(xs, w_in_b)
        ya, conv_s, lru_s = _rglru_decode(us, state_conv[l].reshape(nd, -1), state_lru[l], *lru)
        yb, yc = _decode_attn(l, page_table, caches, us.reshape(nd, 1, -1), lam_vecs, consts, dn_g_t)
        colb = lambda c: jnp.broadcast_to(us[:, c * MIX_W:(c + 1) * MIX_W, None], (nd, MIX_W, HEAD_DIM))
        v_e = jnp.broadcast_to(us[:, COL_HI * MIX_W:(COL_HI + 1) * MIX_W].reshape(nd, N_HEADS, 1, HEAD_DIM),
                               (nd, N_HEADS, HEAD_DIM, HEAD_DIM)).reshape(nd, MIX_W, HEAD_DIM)
        lb_e = jnp.broadcast_to(hg_lb[l][None, :, None], (1, MIX_W, HEAD_DIM))
        hgrn_s, o_d = _hgrn_decode(state_hgrn[l].reshape(nd, MIX_W, HEAD_DIM), colb(COL_HQ), colb(COL_HF), v_e, lb_e)
        yd = _hgrn_decode_out(o_d.reshape(nd, MIX_W), us, hg_g_t)
        x1 = _merge(ya, yb.reshape(nd, MIX_W), yc.reshape(nd, MIX_W), yd, us, xs, *merge_w, alpha)
        xs = _moe(x1, *moe_w, alpha)
        heads = lambda c: us[:, c * MIX_W:(c + 1) * MIX_W].reshape(nd, 1, N_HEADS, HEAD_DIM)
        for lst, val in zip(outs_s, (heads(COL_DK), heads(COL_DV), heads(COL_SK), heads(COL_SV),
                                     conv_s.reshape(nd, CONV_WIDTH - 1, MIX_W), lru_s,
                                     hgrn_s.reshape(nd, N_HEADS, HEAD_DIM, HEAD_DIM))):
            lst.append(val)

    stack = lambda lsts: [jnp.stack(v, axis=0) for v in lsts]
    return (xp.reshape(bsz, seq, d), xs.reshape(nd, 1, d), *stack(outs_p), *stack(outs_s))
```

```python
import functools
import math

import jax
import jax.numpy as jnp
from jax import lax
from jax.experimental import pallas as pl
from jax.experimental.pallas import tpu as pltpu

F32 = jnp.float32
BF16 = jnp.bfloat16

N_BRANCH = 4
MIX_W = 256
N_HEADS = 4
HEAD_DIM = 64
DIFF_QK_DIM = 32
CONV_WIDTH = 4
LRU_C = 8.0
N_GROUPS = 4
EXPERTS_PER_GROUP = 4
N_EXPERTS = 16
LN_EPS = 1e-5
RMS_EPS = 1e-5
NEG_BIG = -1e30
F_FLOOR = 1e-30
PAGE_SIZE = 128
LOG2E = math.log2(math.e)
LANES = 128

COL_XA, COL_GA, COL_DQ, COL_DK, COL_DV, COL_SQ, COL_SK, COL_SV, COL_HQ, COL_HF, COL_HI, COL_HG = range(12)
N_MIX_COLS = 12

V7X_VMEM_BYTES = 64 * 1024 * 1024
VMEM_LIMIT = V7X_VMEM_BYTES - 12 * 1024 * 1024

HG_CHUNK = 64
HG_SUB = 16
ROUTER_LANES = 128
ROUTER_E0 = 16
ROW_CHUNK = 32


def _cparams(*sem):
    return pltpu.CompilerParams(dimension_semantics=sem, vmem_limit_bytes=VMEM_LIMIT)


def _dot(a, b):
    return jnp.dot(a, b, preferred_element_type=F32)


def _dot_nt(a, b):
    return lax.dot_general(a, b, (((1,), (1,)), ((), ())), preferred_element_type=F32)


def _dot_tn(a, b):
    return lax.dot_general(a, b, (((0,), (0,)), ((), ())), preferred_element_type=F32)


def _split_dot(x, w_bf16):
    hi = x.astype(BF16)
    lo = (x - hi.astype(F32)).astype(BF16)
    return _dot(hi, w_bf16) + _dot(lo, w_bf16)


def _sigmoid(x):
    return 1.0 / (1.0 + jnp.exp(-x))


def _silu(x):
    return x * _sigmoid(x)


def _gelu_tanh(x):
    c = math.sqrt(2.0 / math.pi)
    return 0.5 * x * (1.0 + jnp.tanh(c * (x + 0.044715 * (x * x * x))))


def _softplus(x):
    return jnp.maximum(x, 0.0) + jnp.log(1.0 + jnp.exp(-jnp.abs(x)))


def _iota(shape, dim):
    return lax.broadcasted_iota(jnp.int32, shape, dim)


def _head_ones(n):
    return jnp.where((_iota((n, n), 0) // HEAD_DIM) == (_iota((n, n), 1) // HEAD_DIM), 1.0, 0.0)


def _layer_norm(h, g, b):
    mu = jnp.mean(h, axis=-1, keepdims=True)
    d = h - mu
    var = jnp.mean(d * d, axis=-1, keepdims=True)
    return d * lax.rsqrt(var + LN_EPS) * g + b


def _lb_kernel(raw_ref, o_ref):
    raw = raw_ref[...]
    m = jnp.max(raw, axis=0, keepdims=True)
    e = jnp.exp(raw - m)
    soft = e / jnp.sum(e, axis=0, keepdims=True)
    rows, run = [], jnp.zeros_like(soft[0:1, :])
    for l in range(raw.shape[0]):
        run = run + soft[l:l + 1, :]
        rows.append(run)
    cum = jnp.concatenate(rows, axis=0)
    o_ref[...] = jnp.clip(cum - soft[0:1, :], 0.0, 1.0)


def _hgrn_lower_bounds(raw):
    return pl.pallas_call(_lb_kernel, out_shape=jax.ShapeDtypeStruct(raw.shape, F32))(raw)


def _mm_kernel(x_ref, w_ref, o_ref):
    o_ref[...] = _dot(x_ref[...].astype(BF16), w_ref[...])


def _in_proj(x, w):
    n, k = x.shape
    c = w.shape[1]
    tm = min(1024, n)
    tn = 1024
    return pl.pallas_call(
        _mm_kernel,
        grid=(n // tm, c // tn),
        in_specs=[pl.BlockSpec((tm, k), lambda i, j: (i, 0)),
                  pl.BlockSpec((k, tn), lambda i, j: (0, j))],
        out_specs=pl.BlockSpec((tm, tn), lambda i, j: (i, j)),
        out_shape=jax.ShapeDtypeStruct((n, c), F32),
        compiler_params=_cparams("parallel", "parallel"),
    )(x, w)


def _lru_gates(xc, wa, ba, wx, bx, lam):
    xcb = xc.astype(BF16)
    r = _sigmoid(_dot(xcb, wa) + ba)
    i_g = _sigmoid(_dot(xcb, wx) + bx)
    log_a = -LRU_C * r * _softplus(-lam)
    a = jnp.exp(log_a)
    mult = jnp.sqrt(jnp.maximum(1.0 - jnp.exp(2.0 * log_a), 0.0))
    return a, mult, i_g


def _rglru_prompt_kernel(xa_ref, ga_ref, cw_ref, cb_ref, wa_ref, ba_ref, wx_ref, bx_ref, lam_ref,
                         y_ref, conv_ref, h_ref, xbuf, sa, sb, hc):
    t = pl.program_id(1)
    tt = xa_ref.shape[0]
    pad = tt // 2

    @pl.when(t == 0)
    def _():
        xbuf[0:8, :] = jnp.zeros((8, MIX_W), F32)
        hc[...] = jnp.zeros_like(hc)

    sa[0:pad, :] = jnp.ones((pad, MIX_W), F32)
    sb[0:pad, :] = jnp.zeros((pad, MIX_W), F32)

    xa = xa_ref[...]
    xbuf[8:8 + tt, :] = xa
    xc = cb_ref[...] + cw_ref[CONV_WIDTH - 1:CONV_WIDTH, :] * xa
    for i in range(CONV_WIDTH - 1):
        xc = xc + cw_ref[i:i + 1, :] * xbuf[5 + i:5 + i + tt, :]
    a, mult, i_g = _lru_gates(xc, wa_ref[...], ba_ref[...], wx_ref[...], bx_ref[...], lam_ref[...])
    pos = _iota((tt, 1), 0) + t * tt
    mult = jnp.where(pos == 0, 1.0, mult)
    b = mult * i_g * xc
    sa[pad:pad + tt, :] = a
    sb[pad:pad + tt, :] = b
    sb[pad:pad + 1, :] = b[0:1, :] + a[0:1, :] * hc[...]

    d = 1
    while d < tt:
        a_cur = sa[pad:pad + tt, :]
        b_cur = sb[pad:pad + tt, :]
        a_sh = sa[pad - d:pad - d + tt, :]
        b_sh = sb[pad - d:pad - d + tt, :]
        sb[pad:pad + tt, :] = a_cur * b_sh + b_cur
        if 2 * d < tt:
            sa[pad:pad + tt, :] = a_cur * a_sh
        d *= 2

    h = sb[pad:pad + tt, :]
    y_ref[...] = _gelu_tanh(ga_ref[...]) * h
    hc[...] = h[tt - 1:tt, :]
    xbuf[0:8, :] = xa[tt - 8:tt, :]
    conv_ref[...] = xa[tt - (CONV_WIDTH - 1):tt, :]
    h_ref[...] = h[tt - 1:tt, :]


def _rglru_prompt(u3, cw, cb, wa, ba, wx, bx, lam, tt=512):
    bsz, t, _ = u3.shape
    full = lambda shape: pl.BlockSpec(shape, lambda b, i: (0,) * len(shape))
    return pl.pallas_call(
        _rglru_prompt_kernel,
        grid=(bsz, t // tt),
        in_specs=[pl.BlockSpec((None, tt, MIX_W), lambda b, i: (b, i, COL_XA)),
                  pl.BlockSpec((None, tt, MIX_W), lambda b, i: (b, i, COL_GA)),
                  full((CONV_WIDTH, MIX_W)), full((1, MIX_W)), full((MIX_W, MIX_W)), full((1, MIX_W)),
                  full((MIX_W, MIX_W)), full((1, MIX_W)), full((1, MIX_W))],
        out_specs=[pl.BlockSpec((None, tt, MIX_W), lambda b, i: (b, i, 0)),
                   pl.BlockSpec((None, CONV_WIDTH - 1, MIX_W), lambda b, i: (b, 0, 0)),
                   pl.BlockSpec((None, 1, MIX_W), lambda b, i: (b, 0, 0))],
        out_shape=[jax.ShapeDtypeStruct((bsz, t, MIX_W), F32),
                   jax.ShapeDtypeStruct((bsz, CONV_WIDTH - 1, MIX_W), F32),
                   jax.ShapeDtypeStruct((bsz, 1, MIX_W), F32)],
        scratch_shapes=[pltpu.VMEM((tt + 8, MIX_W), F32),
                        pltpu.VMEM((tt + tt // 2, MIX_W), F32),
                        pltpu.VMEM((tt + tt // 2, MIX_W), F32),
                        pltpu.VMEM((1, MIX_W), F32)],
        compiler_params=_cparams("parallel", "arbitrary"),
    )(u3, u3, cw, cb, wa, ba, wx, bx, lam)


def _rglru_decode_kernel(xa_ref, ga_ref, conv_ref, h0_ref, cw_ref, cb_ref, wa_ref, ba_ref, wx_ref, bx_ref,
                         lam_ref, y_ref, convn_ref, h_ref):
    xa = xa_ref[...]
    w = MIX_W
    xc = cb_ref[...] + cw_ref[CONV_WIDTH - 1:CONV_WIDTH, :] * xa
    for i in range(CONV_WIDTH - 1):
        xc = xc + cw_ref[i:i + 1, :] * conv_ref[:, i * w:(i + 1) * w]
    a, mult, i_g = _lru_gates(xc, wa_ref[...], ba_ref[...], wx_ref[...], bx_ref[...], lam_ref[...])
    h = a * h0_ref[...] + mult * i_g * xc
    y_ref[...] = _gelu_tanh(ga_ref[...]) * h
    h_ref[...] = h
    convn_ref[:, 0:(CONV_WIDTH - 2) * w] = conv_ref[:, w:(CONV_WIDTH - 1) * w]
    convn_ref[:, (CONV_WIDTH - 2) * w:(CONV_WIDTH - 1) * w] = xa


def _rglru_decode(u, conv, h0, cw, cb, wa, ba, wx, bx, lam):
    n = u.shape[0]
    cwid = (CONV_WIDTH - 1) * MIX_W
    full = lambda shape: pl.BlockSpec(shape, lambda i: (0,) * len(shape))
    return pl.pallas_call(
        _rglru_decode_kernel,
        grid=(1,),
        in_specs=[pl.BlockSpec((n, MIX_W), lambda i: (0, COL_XA)),
                  pl.BlockSpec((n, MIX_W), lambda i: (0, COL_GA)),
                  full((n, cwid)), full((n, MIX_W)),
                  full((CONV_WIDTH, MIX_W)), full((1, MIX_W)), full((MIX_W, MIX_W)), full((1, MIX_W)),
                  full((MIX_W, MIX_W)), full((1, MIX_W)), full((1, MIX_W))],
        out_specs=[full((n, MIX_W)), full((n, cwid)), full((n, MIX_W))],
        out_shape=[jax.ShapeDtypeStruct((n, MIX_W), F32),
                   jax.ShapeDtypeStruct((n, cwid), F32),
                   jax.ShapeDtypeStruct((n, MIX_W), F32)],
        compiler_params=_cparams("arbitrary"),
    )(u, u, conv, h0, cw, cb, wa, ba, wx, bx, lam)


def _alibi_slope(h):
    return 2.0 ** (-8.0 * (h + 1) / N_HEADS)


def _diff_lambda(lam_ref, cst_ref):
    lv = lam_ref[...]
    s1 = jnp.sum(lv[0:1, :] * lv[1:2, :], axis=1, keepdims=True)
    s2 = jnp.sum(lv[2:3, :] * lv[3:4, :], axis=1, keepdims=True)
    lam_init = cst_ref[0:1, 0:1]
    return jnp.exp(s1) - jnp.exp(s2) + lam_init, lam_init


def _head_slab(h):
    return slice((h // 2) * LANES, (h // 2 + 1) * LANES), (h % 2) * HEAD_DIM


def _diff_q_rows(q, h):
    lane = _iota((1, LANES), 1)
    slab, lo = _head_slab(h)
    qs = q[:, slab] * (DIFF_QK_DIM ** -0.5 * LOG2E)
    q1 = jnp.where((lane >= lo) & (lane < lo + DIFF_QK_DIM), qs, 0.0)
    q2 = jnp.where((lane >= lo + DIFF_QK_DIM) & (lane < lo + HEAD_DIM), qs, 0.0)
    return jnp.concatenate([q1, q2], axis=0).astype(BF16)


def _diff_sum_lane(h):
    return HEAD_DIM if h % 2 == 0 else 0


def _diff_prompt_kernel(q_ref, k_ref, v_ref, lam_ref, cst_ref, g_ref, y_ref, kb_s, va_s, qs_s, m_s, acc_s):
    i = pl.program_id(1)
    tq = q_ref.shape[0]
    tk = tq
    t_all = k_ref.shape[0]
    rows = 2 * tq
    nl = tk // LANES

    @pl.when(i == 0)
    def _():
        lane = _iota((1, LANES), 1)

        def prep(c, carry):
            r0 = pl.multiple_of(c * tk, tk)
            kb_s[pl.ds(r0, tk), :] = k_ref[pl.ds(r0, tk), :].astype(BF16)
            v = v_ref[pl.ds(r0, tk), :]
            for h in range(N_HEADS):
                slab = v[:, (h // 2) * LANES:(h // 2 + 1) * LANES]
                va_s[h, pl.ds(r0, tk), :] = jnp.where(lane == _diff_sum_lane(h), 1.0, slab).astype(BF16)
            return carry

        lax.fori_loop(0, t_all // tk, prep, 0)

    q = q_ref[...]
    for h in range(N_HEADS):
        qs_s[h] = _diff_q_rows(q, h)
    m_s[...] = jnp.full(m_s.shape, NEG_BIG, F32)
    acc_s[...] = jnp.zeros_like(acc_s)
    row_in_tile = _iota((rows, LANES), 0) % tq
    col = _iota((rows, LANES), 1)

    def block(j, masked):
        k0 = pl.multiple_of(j * tk, tk)
        kpos = (_iota((1, tk), 1) + ((j - i) * tk - (tq - 1))).astype(F32)
        for h in range(N_HEADS):
            slab, _ = _head_slab(h)
            s = _dot_nt(qs_s[h], kb_s[pl.ds(k0, tk), slab])
            bias = kpos * (_alibi_slope(h) * LOG2E)
            cols = []
            for c in range(nl):
                sc = s[:, c * LANES:(c + 1) * LANES] + bias[:, c * LANES:(c + 1) * LANES]
                if masked:
                    sc = jnp.where(col + c * LANES <= row_in_tile, sc, NEG_BIG)
                cols.append(sc)
            mx = cols[0]
            for sc in cols[1:]:
                mx = jnp.maximum(mx, sc)
            m_prev = m_s[h]
            m_new = jnp.maximum(m_prev, jnp.max(mx, axis=1, keepdims=True))
            p = jnp.concatenate([jnp.exp2(sc - m_new) for sc in cols], axis=1).astype(BF16)
            acc_s[h] = jnp.exp2(m_prev - m_new) * acc_s[h] + _dot(p, va_s[h, pl.ds(k0, tk), :])
            m_s[h] = m_new

    def body(j, carry):
        block(j, False)
        return carry

    lax.fori_loop(0, i, body, 0)
    block(i, True)
    lam, lam_init = _diff_lambda(lam_ref, cst_ref)
    outs = []
    for h in range(N_HEADS):
        acc = acc_s[h]
        lo = (h % 2) * HEAD_DIM
        sl = _diff_sum_lane(h)
        o = acc[:, lo:lo + HEAD_DIM] / acc[:, sl:sl + 1]
        o = o[0:tq, :] - lam * o[tq:rows, :]
        inv = lax.rsqrt(jnp.mean(o * o, axis=1, keepdims=True) + RMS_EPS)
        outs.append(o * inv * g_ref[...] * (1.0 - lam_init))
    y_ref[...] = jnp.concatenate(outs, axis=1)


def _diff_prompt(u3, lam_vecs, consts, norm_g, tq=512):
    bsz, t, _ = u3.shape
    full = lambda shape: pl.BlockSpec(shape, lambda b, i: (0,) * len(shape))
    return pl.pallas_call(
        _diff_prompt_kernel,
        grid=(bsz, t // tq),
        in_specs=[pl.BlockSpec((None, tq, MIX_W), lambda b, i: (b, i, COL_DQ)),
                  pl.BlockSpec((None, t, MIX_W), lambda b, i: (b, 0, COL_DK)),
                  pl.BlockSpec((None, t, MIX_W), lambda b, i: (b, 0, COL_DV)),
                  full(lam_vecs.shape), full(consts.shape), full(norm_g.shape)],
        out_specs=pl.BlockSpec((None, tq, MIX_W), lambda b, i: (b, i, 0)),
        out_shape=jax.ShapeDtypeStruct((bsz, t, MIX_W), F32),
        scratch_shapes=[pltpu.VMEM((t, MIX_W), BF16), pltpu.VMEM((N_HEADS, t, LANES), BF16),
                        pltpu.VMEM((N_HEADS, 2 * tq, LANES), BF16),
                        pltpu.VMEM((N_HEADS, 2 * tq, LANES), F32), pltpu.VMEM((N_HEADS, 2 * tq, LANES), F32)],
        compiler_params=_cparams("parallel", "arbitrary"),
    )(u3, u3, u3, lam_vecs, consts, norm_g)


def _neg_log2_keep(z2):
    e = jnp.exp2(jnp.minimum(z2, -z2))
    return jnp.maximum(z2, 0.0) + jnp.log2(1.0 + e)


def _strict_upper(n):
    return jnp.where(_iota((n, n), 0) > _iota((n, n), 1), 1.0, 0.0).astype(BF16)


def _sb_prompt_kernel(q_ref, k_ref, v_ref, y_ref, kb_s, vb_s, qs_s, r_s, acc_s, hl_s, tot_s, w_s, *, tk):
    i = pl.program_id(1)
    tq = q_ref.shape[0]
    t_all = k_ref.shape[0]
    nl = tk // LANES
    per_tile = tq // tk

    @pl.when(i == 0)
    def _():
        def prep(c, carry):
            r0 = pl.multiple_of(c * tq, tq)
            kb_s[pl.ds(r0, tq), :] = k_ref[pl.ds(r0, tq), :].astype(BF16)
            vb_s[pl.ds(r0, tq), :] = v_ref[pl.ds(r0, tq), :].astype(BF16)
            return carry

        lax.fori_loop(0, t_all // tq, prep, 0)

    q = q_ref[...]
    lane = _iota((1, LANES), 1)
    for h in range(N_HEADS):
        slab, lo = _head_slab(h)
        qs_s[h] = jnp.where((lane >= lo) & (lane < lo + HEAD_DIM), q[:, slab] * (HEAD_DIM ** -0.5 * LOG2E),
                            0.0).astype(BF16)
    r_s[...] = jnp.zeros_like(r_s)
    acc_s[...] = jnp.zeros_like(acc_s)
    col = _iota((ROW_CHUNK, LANES), 1)
    row0 = _iota((ROW_CHUNK, LANES), 0)
    tri = jnp.where(_iota((2 * tk, tk), 0) % tk >= _iota((2 * tk, tk), 1), 1.0, 0.0).astype(BF16)
    chunks = [slice(r, r + ROW_CHUNK) for r in range(0, tq, ROW_CHUNK)]

    def block(j, key_off):
        k0 = pl.multiple_of(j * tk, tk)

        def earlier(rs, c):
            return col + (c * LANES + key_off) < row0 + rs.start

        z2s = [_dot_nt(qs_s[h], kb_s[pl.ds(k0, tk), _head_slab(h)[0]]) for h in range(N_HEADS)]
        for h in range(N_HEADS):
            for rs in chunks:
                nk = _neg_log2_keep(z2s[h][rs, :])
                if key_off is not None:
                    nk = jnp.concatenate([jnp.where(earlier(rs, c), nk[:, c * LANES:(c + 1) * LANES], 0.0)
                                          for c in range(nl)], axis=1)
                hi = nk.astype(BF16)
                hl_s[h, rs, 0:tk] = hi
                hl_s[h, rs, tk:2 * tk] = (nk - hi.astype(F32)).astype(BF16)
                tot_s[h, rs, :] = jnp.broadcast_to(jnp.sum(nk, axis=1, keepdims=True), (ROW_CHUNK, LANES))
        incl = [_dot(hl_s[h], tri) for h in range(N_HEADS)]
        for h in range(N_HEADS):
            for rs in chunks:
                rr = r_s[h, rs, :]
                z2 = z2s[h][rs, :]
                inc = incl[h][rs, :]
                ws = []
                for c in range(nl):
                    sl = slice(c * LANES, (c + 1) * LANES)
                    wc = jnp.exp2(jnp.minimum(z2[:, sl] - inc[:, sl] - rr, 0.0))
                    if key_off is not None:
                        wc = jnp.where(earlier(rs, c), wc, 0.0)
                    ws.append(wc)
                w_s[h, rs, :] = jnp.concatenate(ws, axis=1).astype(BF16)
                r_s[h, rs, :] = rr + tot_s[h, rs, :]
            acc_s[h] = acc_s[h] + _dot(w_s[h], vb_s[pl.ds(k0, tk), _head_slab(h)[0]])

    for d in range(per_tile - 1, -1, -1):
        block(i * per_tile + d, d * tk)

    def body(jj, carry):
        block(i * per_tile - 1 - jj, None)
        return carry

    lax.fori_loop(0, i * per_tile, body, 0)
    y_ref[...] = jnp.concatenate([acc_s[h][:, _head_slab(h)[1]:_head_slab(h)[1] + HEAD_DIM]
                                  for h in range(N_HEADS)], axis=1)


def _sb_prompt(u3, tq=512, tk=256):
    bsz, t, _ = u3.shape
    return pl.pallas_call(
        functools.partial(_sb_prompt_kernel, tk=tk),
        grid=(bsz, t // tq),
        in_specs=[pl.BlockSpec((None, tq, MIX_W), lambda b, i: (b, i, COL_SQ)),
                  pl.BlockSpec((None, t, MIX_W), lambda b, i: (b, 0, COL_SK)),
                  pl.BlockSpec((None, t, MIX_W), lambda b, i: (b, 0, COL_SV))],
        out_specs=pl.BlockSpec((None, tq, MIX_W), lambda b, i: (b, i, 0)),
        out_shape=jax.ShapeDtypeStruct((bsz, t, MIX_W), F32),
        scratch_shapes=[pltpu.VMEM((t, MIX_W), BF16), pltpu.VMEM((t, MIX_W), BF16),
                        pltpu.VMEM((N_HEADS, tq, LANES), BF16),
                        pltpu.VMEM((N_HEADS, tq, LANES), F32), pltpu.VMEM((N_HEADS, tq, LANES), F32),
                        pltpu.VMEM((N_HEADS, tq, 2 * tk), BF16), pltpu.VMEM((N_HEADS, tq, LANES), F32),
                        pltpu.VMEM((N_HEADS, tq, tk), BF16)],
        compiler_params=_cparams("parallel", "arbitrary"),
    )(u3, u3, u3)


def _decode_attn_one(dk_refs, dv_refs, sk_refs, sv_refs, dq, dkn, dvn, sq, lam_ref, cst_ref, g_ref):
    past = len(dk_refs) * PAGE_SIZE
    w = MIX_W
    lane = _iota((1, w), 1)
    row8 = _iota((2 * N_HEADS, 1), 0)
    cat = lambda page_refs: jnp.concatenate([r[...].astype(BF16) for r in page_refs], axis=1)

    q = dq * (DIFF_QK_DIM ** -0.5 * LOG2E)
    seg = lane // DIFF_QK_DIM
    qrows = jnp.where(seg == row8, q, 0.0)
    slope2 = jnp.exp((row8 // 2 + 1).astype(F32) * (-8.0 * math.log(2.0) / N_HEADS)) * LOG2E
    kpos = _iota((1, past), 1).astype(F32)
    sc = _dot(qrows.astype(BF16), cat(dk_refs)) - slope2 * (float(past) - kpos)
    s_new = jnp.sum(qrows * dkn, axis=1, keepdims=True)
    m = jnp.maximum(s_new, jnp.max(sc, axis=1, keepdims=True))
    p_new = jnp.exp2(s_new - m)
    pr = jnp.exp2(sc - m)
    l = p_new + jnp.sum(pr, axis=1, keepdims=True)
    acc = p_new * dvn + _dot_nt(pr.astype(BF16), cat(dv_refs))
    o = acc / l
    lam, lam_init = _diff_lambda(lam_ref, cst_ref)
    coef = jnp.where(row8 % 2 == 0, 1.0, -lam)
    head_of_lane = lane // HEAD_DIM
    o = jnp.where(head_of_lane == row8 // 2, o * coef, 0.0)
    o = jnp.sum(o, axis=0, keepdims=True)
    ms = _split_dot(o * o, _head_ones(w).astype(BF16)) * (1.0 / HEAD_DIM)
    yb = o * lax.rsqrt(ms + RMS_EPS) * g_ref[...] * (1.0 - lam_init)

    rowh = _iota((N_HEADS, 1), 0)
    qsb = jnp.where(head_of_lane == rowh, sq * (HEAD_DIM ** -0.5 * LOG2E), 0.0).astype(BF16)
    z2 = _dot(qsb, cat(sk_refs))
    nk = _neg_log2_keep(z2)
    chunk = 2 * PAGE_SIZE
    upper = _strict_upper(chunk)
    run = jnp.zeros((N_HEADS, 1), F32)
    later = [None] * (past // chunk)
    for c in range(past // chunk - 1, -1, -1):
        nk_c = nk[:, c * chunk:(c + 1) * chunk]
        later[c] = _split_dot(nk_c, upper) + run
        run = run + jnp.sum(nk_c, axis=1, keepdims=True)
    wgt = jnp.exp2(jnp.minimum(z2 - nk - jnp.concatenate(later, axis=1), 0.0))
    acc = _dot_nt(wgt.astype(BF16), cat(sv_refs))
    yc = jnp.sum(jnp.where(head_of_lane == rowh, acc, 0.0), axis=0, keepdims=True)
    return yb, yc


def _decode_attn_kernel(pt_ref, *refs, n_pages, n_seq):
    del pt_ref
    np_ = n_pages
    (dq_ref, dkn_ref, dvn_ref, sq_ref, lam_ref, cst_ref, g_ref, yb_ref, yc_ref) = refs[4 * np_ * n_seq:]
    for si in range(n_seq):
        pages = [refs[(4 * si + c) * np_:(4 * si + c + 1) * np_] for c in range(4)]
        yb, yc = _decode_attn_one(*pages, dq_ref[si], dkn_ref[si], dvn_ref[si], sq_ref[si],
                                  lam_ref, cst_ref, g_ref)
        yb_ref[si] = yb
        yc_ref[si] = yc


def _decode_attn(layer, page_table, caches, u3, lam_vecs, consts, norm_g_tiled, n_seq=2):
    n, n_pages = page_table.shape

    def page_spec(si, p):
        return pl.BlockSpec((None, None, MIX_W, PAGE_SIZE), lambda b, pt: (layer, pt[b * n_seq + si, p], 0, 0))

    def col_spec(c):
        return pl.BlockSpec((n_seq, 1, MIX_W), lambda b, pt: (b, 0, c))

    full = lambda shape: pl.BlockSpec(shape, lambda b, pt: (0,) * len(shape))
    in_specs = [page_spec(si, p) for si in range(n_seq) for _ in range(4) for p in range(n_pages)]
    in_specs += [col_spec(COL_DQ), col_spec(COL_DK), col_spec(COL_DV), col_spec(COL_SQ),
                 full(lam_vecs.shape), full(consts.shape), full(norm_g_tiled.shape)]
    operands = [c for _ in range(n_seq) for c in caches for _ in range(n_pages)]
    operands += [u3, u3, u3, u3, lam_vecs, consts, norm_g_tiled]
    out_spec = pl.BlockSpec((n_seq, 1, MIX_W), lambda b, pt: (b, 0, 0))
    return pl.pallas_call(
        functools.partial(_decode_attn_kernel, n_pages=n_pages, n_seq=n_seq),
        grid_spec=pltpu.PrefetchScalarGridSpec(
            num_scalar_prefetch=1, grid=(n // n_seq,), in_specs=in_specs, out_specs=[out_spec, out_spec]),
        out_shape=[jax.ShapeDtypeStruct((n, 1, MIX_W), F32), jax.ShapeDtypeStruct((n, 1, MIX_W), F32)],
        compiler_params=_cparams("parallel"),
    )(page_table, *operands)


def _hgrn_gates(hq, hf, lb):
    q = _silu(hq)
    e = jnp.exp(-jnp.abs(hf))
    inv = 1.0 / (1.0 + e)
    pos = hf >= 0.0
    sig = jnp.where(pos, inv, e * inv)
    sig_n = jnp.where(pos, e * inv, inv)
    f = lb + (1.0 - lb) * sig
    return q, f, (1.0 - lb) * sig_n


def _hgrn_out(o, hg, g, ones_bf16):
    ms = _split_dot(o * o, ones_bf16) * (1.0 / HEAD_DIM)
    return o * lax.rsqrt(ms + RMS_EPS) * g * _silu(hg)


def _hgrn_prompt_kernel(hq_ref, hf_ref, hi_ref, hg_ref, lb_ref, g_ref, y_ref, st_ref, st_s):
    t = pl.program_id(1)
    tt = hq_ref.shape[0]
    c = HG_CHUNK
    w = MIX_W

    @pl.when(t == 0)
    def _():
        st_s[...] = jnp.zeros_like(st_s)

    lb = lb_ref[...]
    ones_f = _head_ones(w)
    ones_b = ones_f.astype(BF16)
    tril = jnp.where(_iota((c, c), 1) <= _iota((c, c), 0), 1.0, 0.0).astype(BF16)
    row = _iota((c, 1), 0)
    for ci in range(tt // c):
        sl = slice(ci * c, (ci + 1) * c)
        q, f, k = _hgrn_gates(hq_ref[sl, :], hf_ref[sl, :], lb)
        v = hi_ref[sl, :]
        cum = _split_dot_left(tril, jnp.log(jnp.maximum(f, F_FLOOR)))
        st = st_s[...]
        o = _dot_nt((q * jnp.exp(cum)).astype(BF16), st.astype(BF16))
        for j in range(c // HG_SUB - 1):
            s0, s1 = j * HG_SUB, (j + 1) * HG_SUB
            ref_row = cum[s1 - 1:s1, :]
            qj = q * jnp.exp(jnp.minimum(cum - ref_row, 0.0))
            kj = k[s0:s1, :] * jnp.exp(ref_row - cum[s0:s1, :])
            mt = _dot_tn(v[s0:s1, :].astype(BF16), kj.astype(BF16)) * ones_f
            oj = _dot_nt(qj.astype(BF16), mt.astype(BF16))
            o = o + jnp.where(row >= s1, oj, 0.0)
        for lag in range(HG_SUB):
            if lag == 0:
                ks, cs, vs = k, cum, v
            else:
                ks = pltpu.roll(k, lag, 0)
                cs = pltpu.roll(cum, lag, 0)
                vs = pltpu.roll(v, lag, 0)
            term = q * ks * jnp.exp(jnp.minimum(cum - cs, 0.0))
            ssum = _dot(term.astype(BF16), ones_b)
            o = o + jnp.where(row % HG_SUB >= lag, ssum * vs, 0.0)
        last = cum[c - 1:c, :]
        kc = k * jnp.exp(last - cum)
        st_s[...] = st * jnp.exp(last) + _dot_tn(v.astype(BF16), kc.astype(BF16)) * ones_f
        y_ref[sl, :] = _hgrn_out(o, hg_ref[sl, :], g_ref[...], ones_b)
    st_ref[...] = st_s[...]


def _split_dot_left(w_bf16, x):
    hi = x.astype(BF16)
    lo = (x - hi.astype(F32)).astype(BF16)
    return _dot(w_bf16, hi) + _dot(w_bf16, lo)


def _hgrn_prompt(u3, lb, g_tiled, tt=256):
    bsz, t, _ = u3.shape
    full = lambda shape: pl.BlockSpec(shape, lambda b, i: (0,) * len(shape))
    col = lambda c: pl.BlockSpec((None, tt, MIX_W), lambda b, i: (b, i, c))
    return pl.pallas_call(
        _hgrn_prompt_kernel,
        grid=(bsz, t // tt),
        in_specs=[col(COL_HQ), col(COL_HF), col(COL_HI), col(COL_HG), full((1, MIX_W)), full((1, MIX_W))],
        out_specs=[pl.BlockSpec((None, tt, MIX_W), lambda b, i: (b, i, 0)),
                   pl.BlockSpec((None, MIX_W, MIX_W), lambda b, i: (b, 0, 0))],
        out_shape=[jax.ShapeDtypeStruct((bsz, t, MIX_W), F32),
                   jax.ShapeDtypeStruct((bsz, MIX_W, MIX_W), F32)],
        scratch_shapes=[pltpu.VMEM((MIX_W, MIX_W), F32)],
        compiler_params=_cparams("parallel", "arbitrary"),
    )(u3, u3, u3, u3, lb, g_tiled)


def _hgrn_decode_kernel(s0_ref, q_ref, f_ref, v_ref, lb_ref, sn_ref, o_ref):
    bb = s0_ref.shape[0]
    q, f, k = _hgrn_gates(q_ref[...], f_ref[...], lb_ref[...])
    sn = f * s0_ref[...] + k * v_ref[...]
    sn_ref[...] = sn
    o_ref[...] = jnp.sum((q * sn).reshape(bb, N_HEADS, HEAD_DIM, HEAD_DIM), axis=2)


def _hgrn_decode(s0, q_e, f_e, v_e, lb_e, bb=8):
    n = s0.shape[0]
    blk = pl.BlockSpec((bb, MIX_W, HEAD_DIM), lambda i: (i, 0, 0))
    return pl.pallas_call(
        _hgrn_decode_kernel,
        grid=(n // bb,),
        in_specs=[blk, blk, blk, blk, pl.BlockSpec((1, MIX_W, HEAD_DIM), lambda i: (0, 0, 0))],
        out_specs=[blk, pl.BlockSpec((bb, N_HEADS, HEAD_DIM), lambda i: (i, 0, 0))],
        out_shape=[jax.ShapeDtypeStruct((n, MIX_W, HEAD_DIM), F32),
                   jax.ShapeDtypeStruct((n, N_HEADS, HEAD_DIM), F32)],
        compiler_params=_cparams("parallel"),
    )(s0, q_e, f_e, v_e, lb_e)


def _hgrn_out_kernel(o_ref, hg_ref, g_ref, y_ref):
    y_ref[...] = _hgrn_out(o_ref[...], hg_ref[...], g_ref[...], _head_ones(MIX_W).astype(BF16))


def _hgrn_decode_out(o, u, g_tiled):
    n = o.shape[0]
    full = lambda shape: pl.BlockSpec(shape, lambda i: (0,) * len(shape))
    return pl.pallas_call(
        _hgrn_out_kernel,
        grid=(1,),
        in_specs=[full((n, MIX_W)), pl.BlockSpec((n, MIX_W), lambda i: (0, COL_HG)), full((1, MIX_W))],
        out_specs=full((n, MIX_W)),
        out_shape=jax.ShapeDtypeStruct((n, MIX_W), F32),
        compiler_params=_cparams("arbitrary"),
    )(o, u, g_tiled)


def _merge_kernel(ya_ref, yb_ref, yc_ref, yd_ref, x_ref, wg_ref, wb_ref, wo_ref, lg_ref, lbias_ref, o_ref, *, alpha):
    x = x_ref[...]
    d = x.shape[1]
    xb = x.astype(BF16)
    merged = None
    for n, y_ref in enumerate((ya_ref, yb_ref, yc_ref, yd_ref)):
        gate = _sigmoid(_dot(xb, wg_ref[:, n * d:(n + 1) * d]))
        term = gate * _dot(y_ref[...].astype(BF16), wb_ref[n])
        merged = term if merged is None else merged + term
    out = _dot(merged.astype(BF16), wo_ref[...])
    o_ref[...] = _layer_norm(alpha * x + out, lg_ref[...], lbias_ref[...])


def _merge(ya, yb, yc, yd, x, wg, wb, wo, ln_g, ln_b, alpha):
    n, d = x.shape
    tm = min(512, n)
    row = lambda width: pl.BlockSpec((tm, width), lambda i: (i, 0))
    full = lambda shape: pl.BlockSpec(shape, lambda i: (0,) * len(shape))
    return pl.pallas_call(
        functools.partial(_merge_kernel, alpha=alpha),
        grid=(n // tm,),
        in_specs=[row(MIX_W), row(MIX_W), row(MIX_W), row(MIX_W), row(d),
                  full(wg.shape), full(wb.shape), full(wo.shape), full((1, d)), full((1, d))],
        out_specs=row(d),
        out_shape=jax.ShapeDtypeStruct((n, d), F32),
        compiler_params=_cparams("parallel"),
    )(ya, yb, yc, yd, x, wg, wb, wo, ln_g, ln_b)


def _router_weights(x, wr_hi, wr_lo, br):
    xh = x.astype(BF16)
    xl = (x - xh.astype(F32)).astype(BF16)
    logits = _dot(xh, wr_hi) + _dot(xh, wr_lo) + _dot(xl, wr_hi) + br
    lane = _iota((1, ROUTER_LANES), 1)
    big = jnp.int32(ROUTER_LANES)
    is_g = lane < N_GROUPS
    gl = jnp.where(is_g, logits, NEG_BIG)
    gmax = jnp.max(gl, axis=1, keepdims=True)
    g_idx = jnp.min(jnp.where(is_g & (gl == gmax), lane, big), axis=1, keepdims=True)
    g_w = 1.0 / jnp.sum(jnp.where(is_g, jnp.exp(gl - gmax), 0.0), axis=1, keepdims=True)
    in_grp = (lane >= ROUTER_E0) & (lane < ROUTER_E0 + N_EXPERTS) & \
             ((lane - ROUTER_E0) // EXPERTS_PER_GROUP == g_idx)
    el = jnp.where(in_grp, logits, NEG_BIG)
    v1 = jnp.max(el, axis=1, keepdims=True)
    i1 = jnp.min(jnp.where(in_grp & (el == v1), lane, big), axis=1, keepdims=True)
    el2 = jnp.where(lane == i1, NEG_BIG, el)
    v2 = jnp.max(el2, axis=1, keepdims=True)
    i2 = jnp.min(jnp.where(in_grp & (lane != i1) & (el2 == v2), lane, big), axis=1, keepdims=True)
    e2 = jnp.exp(v2 - v1)
    w1 = g_w / (1.0 + e2)
    w2 = g_w * e2 / (1.0 + e2)
    return jnp.where(lane == i1, w1, 0.0) + jnp.where(lane == i2, w2, 0.0)


def _moe_kernel(x_ref, wrh_ref, wrl_ref, br_ref, w1_ref, w3_ref, w2_ref, lg_ref, lbias_ref, o_ref,
                comb_s, acc_s, xb_s, *, alpha):
    g = pl.program_id(1)
    ff = w1_ref.shape[1] // EXPERTS_PER_GROUP
    tm = x_ref.shape[0]

    @pl.when(g == 0)
    def _():
        x = x_ref[...]
        comb_s[...] = _router_weights(x, wrh_ref[...], wrl_ref[...], br_ref[...])
        xb_s[...] = x.astype(BF16)
        acc_s[...] = jnp.zeros_like(acc_s)

    lane = _iota((1, ROUTER_LANES), 1)
    comb = comb_s[...]
    cexp = []
    for e in range(EXPERTS_PER_GROUP):
        c_e = jnp.sum(jnp.where(lane == ROUTER_E0 + g * EXPERTS_PER_GROUP + e, comb, 0.0), axis=1, keepdims=True)
        cexp.append(jnp.broadcast_to(c_e, (tm, ff)))
    xb = xb_s[...]
    hid = _silu(_dot(xb, w1_ref[...])) * _dot(xb, w3_ref[...]) * jnp.concatenate(cexp, axis=1)
    acc_s[...] += _dot(hid.astype(BF16), w2_ref[...])

    @pl.when(g == pl.num_programs(1) - 1)
    def _():
        o_ref[...] = _layer_norm(alpha * x_ref[...] + acc_s[...], lg_ref[...], lbias_ref[...])


def _moe(x, wr_hi, wr_lo, br, w1g, w3g, w2g, ln_g, ln_b, alpha):
    n, d = x.shape
    ng, _, gff = w1g.shape
    tm = min(1024, n)
    full = lambda shape: pl.BlockSpec(shape, lambda i, g: (0,) * len(shape))
    return pl.pallas_call(
        functools.partial(_moe_kernel, alpha=alpha),
        grid=(n // tm, ng),
        in_specs=[pl.BlockSpec((tm, d), lambda i, g: (i, 0)),
                  full(wr_hi.shape), full(wr_lo.shape), full(br.shape),
                  pl.BlockSpec((None, d, gff), lambda i, g: (g, 0, 0)),
                  pl.BlockSpec((None, d, gff), lambda i, g: (g, 0, 0)),
                  pl.BlockSpec((None, gff, d), lambda i, g: (g, 0, 0)),
                  full((1, d)), full((1, d))],
        out_specs=pl.BlockSpec((tm, d), lambda i, g: (i, 0)),
        out_shape=jax.ShapeDtypeStruct((n, d), F32),
        scratch_shapes=[pltpu.VMEM((tm, ROUTER_LANES), F32), pltpu.VMEM((tm, d), F32),
                        pltpu.VMEM((tm, d), BF16)],
        compiler_params=_cparams("parallel", "arbitrary"),
    )(x, wr_hi, wr_lo, br, w1g, w3g, w2g, ln_g, ln_b)


def _group_expert_weights(w1, w3, w2):
    ne, d, ff = w1.shape
    up = lambda w: w.reshape(N_GROUPS, EXPERTS_PER_GROUP, d, ff).transpose(0, 2, 1, 3).reshape(
        N_GROUPS, d, EXPERTS_PER_GROUP * ff).astype(BF16)
    return up(w1), up(w3), w2.reshape(N_GROUPS, EXPERTS_PER_GROUP * ff, d).astype(BF16)


def _block_diag(w):
    nb, n, _ = w.shape
    eye = jnp.eye(nb, dtype=w.dtype)
    return (eye[:, None, :, None] * w[:, :, None, :]).reshape(nb * n, nb * n)


def _router_matrix(wg, bg, we, be):
    d = wg.shape[0]
    wr = jnp.zeros((d, ROUTER_LANES), F32)
    wr = wr.at[:, 0:N_GROUPS].set(wg).at[:, ROUTER_E0:ROUTER_E0 + N_EXPERTS].set(we)
    br = jnp.zeros((1, ROUTER_LANES), F32)
    br = br.at[0, 0:N_GROUPS].set(bg).at[0, ROUTER_E0:ROUTER_E0 + N_EXPERTS].set(be)
    hi = wr.astype(BF16)
    lo = (wr - hi.astype(F32)).astype(BF16)
    return hi, lo, br


def kernel(x_prompt, x_sample, cache_diff_k, cache_diff_v, cache_sb_k, cache_sb_v, page_table, state_conv, state_lru, state_hgrn, w_in, conv_w, conv_b, lru_wa, lru_ba, lru_wx, lru_bx, lru_lambda, diff_lam_q1, diff_lam_k1, diff_lam_q2, diff_lam_k2, diff_norm_g, hgrn_lb_raw, hgrn_norm_g, w_branch, w_out, ln1_g, ln1_b, router_group_w, router_group_b, router_expert_w, router_expert_b, exp_w1, exp_w3, exp_w2, ln2_g, ln2_b):
    depth = w_in.shape[0]
    bsz, seq, d = x_prompt.shape
    nd = x_sample.shape[0]
    n_pool = cache_diff_k.shape[1]
    alpha = (2 * depth) ** 0.25
    row = lambda a: a.reshape(1, -1)

    hg_lb = _hgrn_lower_bounds(hgrn_lb_raw)
    caches = [jnp.transpose(c, (0, 1, 3, 4, 2)).reshape(depth, n_pool, MIX_W, PAGE_SIZE)
              for c in (cache_diff_k, cache_diff_v, cache_sb_k, cache_sb_v)]

    xp = x_prompt.reshape(bsz * seq, d)
    xs = x_sample.reshape(nd, d)
    outs_p = [[] for _ in range(7)]
    outs_s = [[] for _ in range(7)]
    for l in range(depth):
        n_mix = N_MIX_COLS * MIX_W
        w_in_b = w_in[l][:, :n_mix].astype(BF16)
        w_gate = w_in[l][:, n_mix:].astype(BF16)
        wa = _block_diag(lru_wa[l]).astype(BF16)
        wx = _block_diag(lru_wx[l]).astype(BF16)
        lru = (conv_w[l], row(conv_b[l]), wa, row(lru_ba[l]), wx, row(lru_bx[l]), row(lru_lambda[l]))
        lam_vecs = jnp.stack([diff_lam_q1[l], diff_lam_k1[l], diff_lam_q2[l], diff_lam_k2[l]])
        consts = jnp.zeros((1, 128), F32).at[0, 0].set(0.8 - 0.6 * math.exp(-0.3 * l))
        dn_g = row(diff_norm_g[l])
        dn_g_t = jnp.tile(dn_g, (1, N_HEADS))
        hg_g_t = jnp.tile(row(hgrn_norm_g[l]), (1, N_HEADS))
        lb = row(hg_lb[l])
        wb = w_branch[l].astype(BF16)
        wo = w_out[l].astype(BF16)
        wr_hi, wr_lo, br = _router_matrix(router_group_w[l], router_group_b[l], router_expert_w[l], router_expert_b[l])
        w1, w3, w2 = _group_expert_weights(exp_w1[l], exp_w3[l], exp_w2[l])
        merge_w = (w_gate, wb, wo, row(ln1_g[l]), row(ln1_b[l]))
        moe_w = (wr_hi, wr_lo, br, w1, w3, w2, row(ln2_g[l]), row(ln2_b[l]))

        u = _in_proj(xp, w_in_b)
        u3 = u.reshape(bsz, seq, -1)
        ya, conv_p, lru_p = _rglru_prompt(u3, *lru)
        yb = _diff_prompt(u3, lam_vecs, consts, dn_g)
        yc = _sb_prompt(u3)
        yd, st_t = _hgrn_prompt(u3, lb, hg_g_t)
        flat = lambda a: a.reshape(bsz * seq, MIX_W)
        x1 = _merge(flat(ya), flat(yb), flat(yc), flat(yd), xp, *merge_w, alpha)
        xp = _moe(x1, *moe_w, alpha)
        heads = lambda c: u[:, c * MIX_W:(c + 1) * MIX_W].reshape(bsz, seq, N_HEADS, HEAD_DIM)
        st = jnp.stack([st_t[:, h * HEAD_DIM:(h + 1) * HEAD_DIM, h * HEAD_DIM:(h + 1) * HEAD_DIM]
                        for h in range(N_HEADS)], axis=1).swapaxes(-1, -2)
        for lst, val in zip(outs_p, (heads(COL_DK), heads(COL_DV), heads(COL_SK), heads(COL_SV), conv_p,
                                     lru_p.reshape(bsz, MIX_W), st)):
            lst.append(val)

        us = _in_proj(xs, w_in_b)
        ya, conv_s, lru_s = _rglru_decode(us, state_conv[l].reshape(nd, -1), state_lru[l], *lru)
        yb, yc = _decode_attn(l, page_table, caches, us.reshape(nd, 1, -1), lam_vecs, consts, dn_g_t)
        colb = lambda c: jnp.broadcast_to(us[:, c * MIX_W:(c + 1) * MIX_W, None], (nd, MIX_W, HEAD_DIM))
        v_e = jnp.broadcast_to(us[:, COL_HI * MIX_W:(COL_HI + 1) * MIX_W].reshape(nd, N_HEADS, 1, HEAD_DIM),
                               (nd, N_HEADS, HEAD_DIM, HEAD_DIM)).reshape(nd, MIX_W, HEAD_DIM)
        lb_e = jnp.broadcast_to(hg_lb[l][None, :, None], (1, MIX_W, HEAD_DIM))
        hgrn_s, o_d = _hgrn_decode(state_hgrn[l].reshape(nd, MIX_W, HEAD_DIM), colb(COL_HQ), colb(COL_HF), v_e, lb_e)
        yd = _hgrn_decode_out(o_d.reshape(nd, MIX_W), us, hg_g_t)
        x1 = _merge(ya, yb.reshape(nd, MIX_W), yc.reshape(nd, MIX_W), yd, xs, *merge_w, alpha)
        xs = _moe(x1, *moe_w, alpha)
        heads = lambda c: us[:, c * MIX_W:(c + 1) * MIX_W].reshape(nd, 1, N_HEADS, HEAD_DIM)
        for lst, val in zip(outs_s, (heads(COL_DK), heads(COL_DV), heads(COL_SK), heads(COL_SV),
                                     conv_s.reshape(nd, CONV_WIDTH - 1, MIX_W), lru_s,
                                     hgrn_s.reshape(nd, N_HEADS, HEAD_DIM, HEAD_DIM))):
            lst.append(val)

    stack = lambda lsts: [jnp.stack(v, axis=0) for v in lsts]
    return (xp.reshape(bsz, seq, d), xs.reshape(nd, 1, d), *stack(outs_p), *stack(outs_s))
```

```python
import functools
import math

import jax
import jax.numpy as jnp
from jax import lax
from jax.experimental import pallas as pl
from jax.experimental.pallas import tpu as pltpu

F32 = jnp.float32
BF16 = jnp.bfloat16

N_BRANCH = 4
MIX_W = 256
N_HEADS = 4
HEAD_DIM = 64
DIFF_QK_DIM = 32
CONV_WIDTH = 4
LRU_C = 8.0
N_GROUPS = 4
EXPERTS_PER_GROUP = 4
N_EXPERTS = 16
LN_EPS = 1e-5
RMS_EPS = 1e-5
NEG_BIG = -1e30
F_FLOOR = 1e-30
PAGE_SIZE = 128
LOG2E = math.log2(math.e)
LANES = 128

COL_XA, COL_GA, COL_DQ, COL_DK, COL_DV, COL_SQ, COL_SK, COL_SV, COL_HQ, COL_HF, COL_HI, COL_HG = range(12)
N_MIX_COLS = 12

V7X_VMEM_BYTES = 64 * 1024 * 1024
VMEM_LIMIT = V7X_VMEM_BYTES - 12 * 1024 * 1024

HG_CHUNK = 64
HG_SUB = 16
ROUTER_LANES = 128
ROUTER_E0 = 16
ROW_CHUNK = 32


def _cparams(*sem):
    return pltpu.CompilerParams(dimension_semantics=sem, vmem_limit_bytes=VMEM_LIMIT)


def _dot(a, b):
    return jnp.dot(a, b, preferred_element_type=F32)


def _dot_nt(a, b):
    return lax.dot_general(a, b, (((1,), (1,)), ((), ())), preferred_element_type=F32)


def _dot_tn(a, b):
    return lax.dot_general(a, b, (((0,), (0,)), ((), ())), preferred_element_type=F32)


def _split_dot(x, w_bf16):
    hi = x.astype(BF16)
    lo = (x - hi.astype(F32)).astype(BF16)
    return _dot(hi, w_bf16) + _dot(lo, w_bf16)


def _sigmoid(x):
    return 1.0 / (1.0 + jnp.exp(-x))


def _silu(x):
    return x * _sigmoid(x)


def _gelu_tanh(x):
    c = math.sqrt(2.0 / math.pi)
    return 0.5 * x * (1.0 + jnp.tanh(c * (x + 0.044715 * (x * x * x))))


def _softplus(x):
    return jnp.maximum(x, 0.0) + jnp.log(1.0 + jnp.exp(-jnp.abs(x)))


def _iota(shape, dim):
    return lax.broadcasted_iota(jnp.int32, shape, dim)


def _head_ones(n):
    return jnp.where((_iota((n, n), 0) // HEAD_DIM) == (_iota((n, n), 1) // HEAD_DIM), 1.0, 0.0)


def _layer_norm(h, g, b):
    mu = jnp.mean(h, axis=-1, keepdims=True)
    d = h - mu
    var = jnp.mean(d * d, axis=-1, keepdims=True)
    return d * lax.rsqrt(var + LN_EPS) * g + b


def _lb_kernel(raw_ref, o_ref):
    raw = raw_ref[...]
    m = jnp.max(raw, axis=0, keepdims=True)
    e = jnp.exp(raw - m)
    soft = e / jnp.sum(e, axis=0, keepdims=True)
    rows, run = [], jnp.zeros_like(soft[0:1, :])
    for l in range(raw.shape[0]):
        run = run + soft[l:l + 1, :]
        rows.append(run)
    cum = jnp.concatenate(rows, axis=0)
    o_ref[...] = jnp.clip(cum - soft[0:1, :], 0.0, 1.0)


def _hgrn_lower_bounds(raw):
    return pl.pallas_call(_lb_kernel, out_shape=jax.ShapeDtypeStruct(raw.shape, F32))(raw)


def _mm_kernel(x_ref, w_ref, o_ref):
    o_ref[...] = _dot(x_ref[...].astype(BF16), w_ref[...])


def _in_proj(x, w):
    n, k = x.shape
    c = N_MIX_COLS * MIX_W
    tm = min(1024, n)
    tn = 1024
    return pl.pallas_call(
        _mm_kernel,
        grid=(n // tm, c // tn),
        in_specs=[pl.BlockSpec((tm, k), lambda i, j: (i, 0)),
                  pl.BlockSpec((k, tn), lambda i, j: (0, j))],
        out_specs=pl.BlockSpec((tm, tn), lambda i, j: (i, j)),
        out_shape=jax.ShapeDtypeStruct((n, c), F32),
        compiler_params=_cparams("parallel", "parallel"),
    )(x, w)


def _lru_gates(xc, wa, ba, wx, bx, lam):
    xcb = xc.astype(BF16)
    r = _sigmoid(_dot(xcb, wa) + ba)
    i_g = _sigmoid(_dot(xcb, wx) + bx)
    log_a = -LRU_C * r * _softplus(-lam)
    a = jnp.exp(log_a)
    mult = jnp.sqrt(jnp.maximum(1.0 - jnp.exp(2.0 * log_a), 0.0))
    return a, mult, i_g


def _rglru_prompt_kernel(xa_ref, ga_ref, cw_ref, cb_ref, wa_ref, ba_ref, wx_ref, bx_ref, lam_ref,
                         y_ref, conv_ref, h_ref, xbuf, sa, sb, hc):
    t = pl.program_id(1)
    tt = xa_ref.shape[0]
    pad = tt // 2

    @pl.when(t == 0)
    def _():
        xbuf[0:8, :] = jnp.zeros((8, MIX_W), F32)
        hc[...] = jnp.zeros_like(hc)

    sa[0:pad, :] = jnp.ones((pad, MIX_W), F32)
    sb[0:pad, :] = jnp.zeros((pad, MIX_W), F32)

    xa = xa_ref[...]
    xbuf[8:8 + tt, :] = xa
    xc = cb_ref[...] + cw_ref[CONV_WIDTH - 1:CONV_WIDTH, :] * xa
    for i in range(CONV_WIDTH - 1):
        xc = xc + cw_ref[i:i + 1, :] * xbuf[5 + i:5 + i + tt, :]
    a, mult, i_g = _lru_gates(xc, wa_ref[...], ba_ref[...], wx_ref[...], bx_ref[...], lam_ref[...])
    pos = _iota((tt, 1), 0) + t * tt
    mult = jnp.where(pos == 0, 1.0, mult)
    b = mult * i_g * xc
    sa[pad:pad + tt, :] = a
    sb[pad:pad + tt, :] = b
    sb[pad:pad + 1, :] = b[0:1, :] + a[0:1, :] * hc[...]

    d = 1
    while d < tt:
        a_cur = sa[pad:pad + tt, :]
        b_cur = sb[pad:pad + tt, :]
        a_sh = sa[pad - d:pad - d + tt, :]
        b_sh = sb[pad - d:pad - d + tt, :]
        sb[pad:pad + tt, :] = a_cur * b_sh + b_cur
        if 2 * d < tt:
            sa[pad:pad + tt, :] = a_cur * a_sh
        d *= 2

    h = sb[pad:pad + tt, :]
    y_ref[...] = _gelu_tanh(ga_ref[...]) * h
    hc[...] = h[tt - 1:tt, :]
    xbuf[0:8, :] = xa[tt - 8:tt, :]
    conv_ref[...] = xa[tt - (CONV_WIDTH - 1):tt, :]
    h_ref[...] = h[tt - 1:tt, :]


def _rglru_prompt(u3, cw, cb, wa, ba, wx, bx, lam, tt=512):
    bsz, t, _ = u3.shape
    full = lambda shape: pl.BlockSpec(shape, lambda b, i: (0,) * len(shape))
    return pl.pallas_call(
        _rglru_prompt_kernel,
        grid=(bsz, t // tt),
        in_specs=[pl.BlockSpec((None, tt, MIX_W), lambda b, i: (b, i, COL_XA)),
                  pl.BlockSpec((None, tt, MIX_W), lambda b, i: (b, i, COL_GA)),
                  full((CONV_WIDTH, MIX_W)), full((1, MIX_W)), full((MIX_W, MIX_W)), full((1, MIX_W)),
                  full((MIX_W, MIX_W)), full((1, MIX_W)), full((1, MIX_W))],
        out_specs=[pl.BlockSpec((None, tt, MIX_W), lambda b, i: (b, i, 0)),
                   pl.BlockSpec((None, CONV_WIDTH - 1, MIX_W), lambda b, i: (b, 0, 0)),
                   pl.BlockSpec((None, 1, MIX_W), lambda b, i: (b, 0, 0))],
        out_shape=[jax.ShapeDtypeStruct((bsz, t, MIX_W), F32),
                   jax.ShapeDtypeStruct((bsz, CONV_WIDTH - 1, MIX_W), F32),
                   jax.ShapeDtypeStruct((bsz, 1, MIX_W), F32)],
        scratch_shapes=[pltpu.VMEM((tt + 8, MIX_W), F32),
                        pltpu.VMEM((tt + tt // 2, MIX_W), F32),
                        pltpu.VMEM((tt + tt // 2, MIX_W), F32),
                        pltpu.VMEM((1, MIX_W), F32)],
        compiler_params=_cparams("parallel", "arbitrary"),
    )(u3, u3, cw, cb, wa, ba, wx, bx, lam)


def _rglru_decode_kernel(xa_ref, ga_ref, conv_ref, h0_ref, cw_ref, cb_ref, wa_ref, ba_ref, wx_ref, bx_ref,
                         lam_ref, y_ref, convn_ref, h_ref):
    xa = xa_ref[...]
    w = MIX_W
    xc = cb_ref[...] + cw_ref[CONV_WIDTH - 1:CONV_WIDTH, :] * xa
    for i in range(CONV_WIDTH - 1):
        xc = xc + cw_ref[i:i + 1, :] * conv_ref[:, i * w:(i + 1) * w]
    a, mult, i_g = _lru_gates(xc, wa_ref[...], ba_ref[...], wx_ref[...], bx_ref[...], lam_ref[...])
    h = a * h0_ref[...] + mult * i_g * xc
    y_ref[...] = _gelu_tanh(ga_ref[...]) * h
    h_ref[...] = h
    convn_ref[:, 0:(CONV_WIDTH - 2) * w] = conv_ref[:, w:(CONV_WIDTH - 1) * w]
    convn_ref[:, (CONV_WIDTH - 2) * w:(CONV_WIDTH - 1) * w] = xa


def _rglru_decode(u, conv, h0, cw, cb, wa, ba, wx, bx, lam):
    n = u.shape[0]
    cwid = (CONV_WIDTH - 1) * MIX_W
    full = lambda shape: pl.BlockSpec(shape, lambda i: (0,) * len(shape))
    return pl.pallas_call(
        _rglru_decode_kernel,
        grid=(1,),
        in_specs=[pl.BlockSpec((n, MIX_W), lambda i: (0, COL_XA)),
                  pl.BlockSpec((n, MIX_W), lambda i: (0, COL_GA)),
                  full((n, cwid)), full((n, MIX_W)),
                  full((CONV_WIDTH, MIX_W)), full((1, MIX_W)), full((MIX_W, MIX_W)), full((1, MIX_W)),
                  full((MIX_W, MIX_W)), full((1, MIX_W)), full((1, MIX_W))],
        out_specs=[full((n, MIX_W)), full((n, cwid)), full((n, MIX_W))],
        out_shape=[jax.ShapeDtypeStruct((n, MIX_W), F32),
                   jax.ShapeDtypeStruct((n, cwid), F32),
                   jax.ShapeDtypeStruct((n, MIX_W), F32)],
        compiler_params=_cparams("arbitrary"),
    )(u, u, conv, h0, cw, cb, wa, ba, wx, bx, lam)


def _alibi_slope(h):
    return 2.0 ** (-8.0 * (h + 1) / N_HEADS)


def _diff_lambda(lam_ref, cst_ref):
    lv = lam_ref[...]
    s1 = jnp.sum(lv[0:1, :] * lv[1:2, :], axis=1, keepdims=True)
    s2 = jnp.sum(lv[2:3, :] * lv[3:4, :], axis=1, keepdims=True)
    lam_init = cst_ref[0:1, 0:1]
    return jnp.exp(s1) - jnp.exp(s2) + lam_init, lam_init


def _head_slab(h):
    return slice((h // 2) * LANES, (h // 2 + 1) * LANES), (h % 2) * HEAD_DIM


def _diff_q_rows(q, h):
    lane = _iota((1, LANES), 1)
    slab, lo = _head_slab(h)
    qs = q[:, slab] * (DIFF_QK_DIM ** -0.5 * LOG2E)
    q1 = jnp.where((lane >= lo) & (lane < lo + DIFF_QK_DIM), qs, 0.0)
    q2 = jnp.where((lane >= lo + DIFF_QK_DIM) & (lane < lo + HEAD_DIM), qs, 0.0)
    return jnp.concatenate([q1, q2], axis=0).astype(BF16)


def _diff_sum_lane(h):
    return HEAD_DIM if h % 2 == 0 else 0


def _diff_prompt_kernel(q_ref, k_ref, v_ref, lam_ref, cst_ref, g_ref, y_ref, kb_s, va_s, qs_s, m_s, acc_s):
    i = pl.program_id(1)
    tq = q_ref.shape[0]
    tk = tq
    t_all = k_ref.shape[0]
    rows = 2 * tq
    nl = tk // LANES

    @pl.when(i == 0)
    def _():
        lane = _iota((1, LANES), 1)

        def prep(c, carry):
            r0 = pl.multiple_of(c * tk, tk)
            kb_s[pl.ds(r0, tk), :] = k_ref[pl.ds(r0, tk), :].astype(BF16)
            v = v_ref[pl.ds(r0, tk), :]
            for h in range(N_HEADS):
                slab = v[:, (h // 2) * LANES:(h // 2 + 1) * LANES]
                va_s[h, pl.ds(r0, tk), :] = jnp.where(lane == _diff_sum_lane(h), 1.0, slab).astype(BF16)
            return carry

        lax.fori_loop(0, t_all // tk, prep, 0)

    q = q_ref[...]
    for h in range(N_HEADS):
        qs_s[h] = _diff_q_rows(q, h)
    m_s[...] = jnp.full(m_s.shape, NEG_BIG, F32)
    acc_s[...] = jnp.zeros_like(acc_s)
    row_in_tile = _iota((rows, LANES), 0) % tq
    col = _iota((rows, LANES), 1)

    def block(j, masked):
        k0 = pl.multiple_of(j * tk, tk)
        kpos = (_iota((1, tk), 1) + ((j - i) * tk - (tq - 1))).astype(F32)
        for h in range(N_HEADS):
            slab, _ = _head_slab(h)
            s = _dot_nt(qs_s[h], kb_s[pl.ds(k0, tk), slab])
            bias = kpos * (_alibi_slope(h) * LOG2E)
            cols = []
            for c in range(nl):
                sc = s[:, c * LANES:(c + 1) * LANES] + bias[:, c * LANES:(c + 1) * LANES]
                if masked:
                    sc = jnp.where(col + c * LANES <= row_in_tile, sc, NEG_BIG)
                cols.append(sc)
            mx = cols[0]
            for sc in cols[1:]:
                mx = jnp.maximum(mx, sc)
            m_prev = m_s[h]
            m_new = jnp.maximum(m_prev, jnp.max(mx, axis=1, keepdims=True))
            p = jnp.concatenate([jnp.exp2(sc - m_new) for sc in cols], axis=1).astype(BF16)
            acc_s[h] = jnp.exp2(m_prev - m_new) * acc_s[h] + _dot(p, va_s[h, pl.ds(k0, tk), :])
            m_s[h] = m_new

    def body(j, carry):
        block(j, False)
        return carry

    lax.fori_loop(0, i, body, 0)
    block(i, True)
    lam, lam_init = _diff_lambda(lam_ref, cst_ref)
    outs = []
    for h in range(N_HEADS):
        acc = acc_s[h]
        lo = (h % 2) * HEAD_DIM
        sl = _diff_sum_lane(h)
        o = acc[:, lo:lo + HEAD_DIM] / acc[:, sl:sl + 1]
        o = o[0:tq, :] - lam * o[tq:rows, :]
        inv = lax.rsqrt(jnp.mean(o * o, axis=1, keepdims=True) + RMS_EPS)
        outs.append(o * inv * g_ref[...] * (1.0 - lam_init))
    y_ref[...] = jnp.concatenate(outs, axis=1)


def _diff_prompt(u3, lam_vecs, consts, norm_g, tq=512):
    bsz, t, _ = u3.shape
    full = lambda shape: pl.BlockSpec(shape, lambda b, i: (0,) * len(shape))
    return pl.pallas_call(
        _diff_prompt_kernel,
        grid=(bsz, t // tq),
        in_specs=[pl.BlockSpec((None, tq, MIX_W), lambda b, i: (b, i, COL_DQ)),
                  pl.BlockSpec((None, t, MIX_W), lambda b, i: (b, 0, COL_DK)),
                  pl.BlockSpec((None, t, MIX_W), lambda b, i: (b, 0, COL_DV)),
                  full(lam_vecs.shape), full(consts.shape), full(norm_g.shape)],
        out_specs=pl.BlockSpec((None, tq, MIX_W), lambda b, i: (b, i, 0)),
        out_shape=jax.ShapeDtypeStruct((bsz, t, MIX_W), F32),
        scratch_shapes=[pltpu.VMEM((t, MIX_W), BF16), pltpu.VMEM((N_HEADS, t, LANES), BF16),
                        pltpu.VMEM((N_HEADS, 2 * tq, LANES), BF16),
                        pltpu.VMEM((N_HEADS, 2 * tq, LANES), F32), pltpu.VMEM((N_HEADS, 2 * tq, LANES), F32)],
        compiler_params=_cparams("parallel", "arbitrary"),
    )(u3, u3, u3, lam_vecs, consts, norm_g)


def _neg_log2_keep(z2):
    e = jnp.exp2(jnp.minimum(z2, -z2))
    return jnp.maximum(z2, 0.0) + jnp.log2(1.0 + e)


def _strict_upper(n):
    return jnp.where(_iota((n, n), 0) > _iota((n, n), 1), 1.0, 0.0).astype(BF16)


def _sb_prompt_kernel(q_ref, k_ref, v_ref, y_ref, kb_s, vb_s, qs_s, r_s, acc_s, hl_s, tot_s, w_s, *, tk):
    i = pl.program_id(1)
    tq = q_ref.shape[0]
    t_all = k_ref.shape[0]
    nl = tk // LANES
    per_tile = tq // tk

    @pl.when(i == 0)
    def _():
        def prep(c, carry):
            r0 = pl.multiple_of(c * tq, tq)
            kb_s[pl.ds(r0, tq), :] = k_ref[pl.ds(r0, tq), :].astype(BF16)
            vb_s[pl.ds(r0, tq), :] = v_ref[pl.ds(r0, tq), :].astype(BF16)
            return carry

        lax.fori_loop(0, t_all // tq, prep, 0)

    q = q_ref[...]
    lane = _iota((1, LANES), 1)
    for h in range(N_HEADS):
        slab, lo = _head_slab(h)
        qs_s[h] = jnp.where((lane >= lo) & (lane < lo + HEAD_DIM), q[:, slab] * (HEAD_DIM ** -0.5 * LOG2E),
                            0.0).astype(BF16)
    r_s[...] = jnp.zeros_like(r_s)
    acc_s[...] = jnp.zeros_like(acc_s)
    col = _iota((ROW_CHUNK, LANES), 1)
    row0 = _iota((ROW_CHUNK, LANES), 0)
    tri = jnp.where(_iota((2 * tk, tk), 0) % tk >= _iota((2 * tk, tk), 1), 1.0, 0.0).astype(BF16)
    chunks = [slice(r, r + ROW_CHUNK) for r in range(0, tq, ROW_CHUNK)]

    def block(j, key_off):
        k0 = pl.multiple_of(j * tk, tk)

        def earlier(rs, c):
            return col + (c * LANES + key_off) < row0 + rs.start

        z2s = [_dot_nt(qs_s[h], kb_s[pl.ds(k0, tk), _head_slab(h)[0]]) for h in range(N_HEADS)]
        for h in range(N_HEADS):
            for rs in chunks:
                nk = _neg_log2_keep(z2s[h][rs, :])
                if key_off is not None:
                    nk = jnp.concatenate([jnp.where(earlier(rs, c), nk[:, c * LANES:(c + 1) * LANES], 0.0)
                                          for c in range(nl)], axis=1)
                hi = nk.astype(BF16)
                hl_s[h, rs, 0:tk] = hi
                hl_s[h, rs, tk:2 * tk] = (nk - hi.astype(F32)).astype(BF16)
                tot_s[h, rs, :] = jnp.broadcast_to(jnp.sum(nk, axis=1, keepdims=True), (ROW_CHUNK, LANES))
        incl = [_dot(hl_s[h], tri) for h in range(N_HEADS)]
        for h in range(N_HEADS):
            for rs in chunks:
                rr = r_s[h, rs, :]
                z2 = z2s[h][rs, :]
                inc = incl[h][rs, :]
                ws = []
                for c in range(nl):
                    sl = slice(c * LANES, (c + 1) * LANES)
                    wc = jnp.exp2(jnp.minimum(z2[:, sl] - inc[:, sl] - rr, 0.0))
                    if key_off is not None:
                        wc = jnp.where(earlier(rs, c), wc, 0.0)
                    ws.append(wc)
                w_s[h, rs, :] = jnp.concatenate(ws, axis=1).astype(BF16)
                r_s[h, rs, :] = rr + tot_s[h, rs, :]
            acc_s[h] = acc_s[h] + _dot(w_s[h], vb_s[pl.ds(k0, tk), _head_slab(h)[0]])

    for d in range(per_tile - 1, -1, -1):
        block(i * per_tile + d, d * tk)

    def body(jj, carry):
        block(i * per_tile - 1 - jj, None)
        return carry

    lax.fori_loop(0, i * per_tile, body, 0)
    y_ref[...] = jnp.concatenate([acc_s[h][:, _head_slab(h)[1]:_head_slab(h)[1] + HEAD_DIM]
                                  for h in range(N_HEADS)], axis=1)


def _sb_prompt(u3, tq=512, tk=256):
    bsz, t, _ = u3.shape
    return pl.pallas_call(
        functools.partial(_sb_prompt_kernel, tk=tk),
        grid=(bsz, t // tq),
        in_specs=[pl.BlockSpec((None, tq, MIX_W), lambda b, i: (b, i, COL_SQ)),
                  pl.BlockSpec((None, t, MIX_W), lambda b, i: (b, 0, COL_SK)),
                  pl.BlockSpec((None, t, MIX_W), lambda b, i: (b, 0, COL_SV))],
        out_specs=pl.BlockSpec((None, tq, MIX_W), lambda b, i: (b, i, 0)),
        out_shape=jax.ShapeDtypeStruct((bsz, t, MIX_W), F32),
        scratch_shapes=[pltpu.VMEM((t, MIX_W), BF16), pltpu.VMEM((t, MIX_W), BF16),
                        pltpu.VMEM((N_HEADS, tq, LANES), BF16),
                        pltpu.VMEM((N_HEADS, tq, LANES), F32), pltpu.VMEM((N_HEADS, tq, LANES), F32),
                        pltpu.VMEM((N_HEADS, tq, 2 * tk), BF16), pltpu.VMEM((N_HEADS, tq, LANES), F32),
                        pltpu.VMEM((N_HEADS, tq, tk), BF16)],
        compiler_params=_cparams("parallel", "arbitrary"),
    )(u3, u3, u3)


def _decode_attn_one(dk_refs, dv_refs, sk_refs, sv_refs, dq, dkn, dvn, sq, lam_ref, cst_ref, g_ref):
    past = len(dk_refs) * PAGE_SIZE
    w = MIX_W
    lane = _iota((1, w), 1)
    row8 = _iota((2 * N_HEADS, 1), 0)
    cat = lambda page_refs: jnp.concatenate([r[...].astype(BF16) for r in page_refs], axis=1)

    q = dq * (DIFF_QK_DIM ** -0.5 * LOG2E)
    seg = lane // DIFF_QK_DIM
    qrows = jnp.where(seg == row8, q, 0.0)
    slope2 = jnp.exp((row8 // 2 + 1).astype(F32) * (-8.0 * math.log(2.0) / N_HEADS)) * LOG2E
    kpos = _iota((1, past), 1).astype(F32)
    sc = _dot(qrows.astype(BF16), cat(dk_refs)) - slope2 * (float(past) - kpos)
    s_new = jnp.sum(qrows * dkn, axis=1, keepdims=True)
    m = jnp.maximum(s_new, jnp.max(sc, axis=1, keepdims=True))
    p_new = jnp.exp2(s_new - m)
    pr = jnp.exp2(sc - m)
    l = p_new + jnp.sum(pr, axis=1, keepdims=True)
    acc = p_new * dvn + _dot_nt(pr.astype(BF16), cat(dv_refs))
    o = acc / l
    lam, lam_init = _diff_lambda(lam_ref, cst_ref)
    coef = jnp.where(row8 % 2 == 0, 1.0, -lam)
    head_of_lane = lane // HEAD_DIM
    o = jnp.where(head_of_lane == row8 // 2, o * coef, 0.0)
    o = jnp.sum(o, axis=0, keepdims=True)
    ms = _split_dot(o * o, _head_ones(w).astype(BF16)) * (1.0 / HEAD_DIM)
    yb = o * lax.rsqrt(ms + RMS_EPS) * g_ref[...] * (1.0 - lam_init)

    rowh = _iota((N_HEADS, 1), 0)
    qsb = jnp.where(head_of_lane == rowh, sq * (HEAD_DIM ** -0.5 * LOG2E), 0.0).astype(BF16)
    z2 = _dot(qsb, cat(sk_refs))
    nk = _neg_log2_keep(z2)
    chunk = 2 * PAGE_SIZE
    upper = _strict_upper(chunk)
    run = jnp.zeros((N_HEADS, 1), F32)
    later = [None] * (past // chunk)
    for c in range(past // chunk - 1, -1, -1):
        nk_c = nk[:, c * chunk:(c + 1) * chunk]
        later[c] = _split_dot(nk_c, upper) + run
        run = run + jnp.sum(nk_c, axis=1, keepdims=True)
    wgt = jnp.exp2(jnp.minimum(z2 - nk - jnp.concatenate(later, axis=1), 0.0))
    acc = _dot_nt(wgt.astype(BF16), cat(sv_refs))
    yc = jnp.sum(jnp.where(head_of_lane == rowh, acc, 0.0), axis=0, keepdims=True)
    return yb, yc


def _decode_attn_kernel(pt_ref, *refs, n_pages, n_seq):
    del pt_ref
    np_ = n_pages
    (dq_ref, dkn_ref, dvn_ref, sq_ref, lam_ref, cst_ref, g_ref, yb_ref, yc_ref) = refs[4 * np_ * n_seq:]
    for si in range(n_seq):
        pages = [refs[(4 * si + c) * np_:(4 * si + c + 1) * np_] for c in range(4)]
        yb, yc = _decode_attn_one(*pages, dq_ref[si], dkn_ref[si], dvn_ref[si], sq_ref[si],
                                  lam_ref, cst_ref, g_ref)
        yb_ref[si] = yb
        yc_ref[si] = yc


def _decode_attn(layer, page_table, caches, u3, lam_vecs, consts, norm_g_tiled, n_seq=2):
    n, n_pages = page_table.shape

    def page_spec(si, p):
        return pl.BlockSpec((None, None, MIX_W, PAGE_SIZE), lambda b, pt: (layer, pt[b * n_seq + si, p], 0, 0))

    def col_spec(c):
        return pl.BlockSpec((n_seq, 1, MIX_W), lambda b, pt: (b, 0, c))

    full = lambda shape: pl.BlockSpec(shape, lambda b, pt: (0,) * len(shape))
    in_specs = [page_spec(si, p) for si in range(n_seq) for _ in range(4) for p in range(n_pages)]
    in_specs += [col_spec(COL_DQ), col_spec(COL_DK), col_spec(COL_DV), col_spec(COL_SQ),
                 full(lam_vecs.shape), full(consts.shape), full(norm_g_tiled.shape)]
    operands = [c for _ in range(n_seq) for c in caches for _ in range(n_pages)]
    operands += [u3, u3, u3, u3, lam_vecs, consts, norm_g_tiled]
    out_spec = pl.BlockSpec((n_seq, 1, MIX_W), lambda b, pt: (b, 0, 0))
    return pl.pallas_call(
        functools.partial(_decode_attn_kernel, n_pages=n_pages, n_seq=n_seq),
        grid_spec=pltpu.PrefetchScalarGridSpec(
            num_scalar_prefetch=1, grid=(n // n_seq,), in_specs=in_specs, out_specs=[out_spec, out_spec]),
        out_shape=[jax.ShapeDtypeStruct((n, 1, MIX_W), F32), jax.ShapeDtypeStruct((n, 1, MIX_W), F32)],
        compiler_params=_cparams("parallel"),
    )(page_table, *operands)


def _hgrn_gates(hq, hf, lb):
    q = _silu(hq)
    e = jnp.exp(-jnp.abs(hf))
    inv = 1.0 / (1.0 + e)
    pos = hf >= 0.0
    sig = jnp.where(pos, inv, e * inv)
    sig_n = jnp.where(pos, e * inv, inv)
    f = lb + (1.0 - lb) * sig
    return q, f, (1.0 - lb) * sig_n


def _hgrn_out(o, hg, g, ones_bf16):
    ms = _split_dot(o * o, ones_bf16) * (1.0 / HEAD_DIM)
    return o * lax.rsqrt(ms + RMS_EPS) * g * _silu(hg)


def _hgrn_prompt_kernel(hq_ref, hf_ref, hi_ref, hg_ref, lb_ref, g_ref, y_ref, st_ref, st_s):
    t = pl.program_id(1)
    tt = hq_ref.shape[0]
    c = HG_CHUNK
    w = MIX_W

    @pl.when(t == 0)
    def _():
        st_s[...] = jnp.zeros_like(st_s)

    lb = lb_ref[...]
    ones_f = _head_ones(w)
    ones_b = ones_f.astype(BF16)
    tril = jnp.where(_iota((c, c), 1) <= _iota((c, c), 0), 1.0, 0.0).astype(BF16)
    row = _iota((c, 1), 0)
    for ci in range(tt // c):
        sl = slice(ci * c, (ci + 1) * c)
        q, f, k = _hgrn_gates(hq_ref[sl, :], hf_ref[sl, :], lb)
        v = hi_ref[sl, :]
        cum = _split_dot_left(tril, jnp.log(jnp.maximum(f, F_FLOOR)))
        st = st_s[...]
        o = _dot_nt((q * jnp.exp(cum)).astype(BF16), st.astype(BF16))
        for j in range(c // HG_SUB - 1):
            s0, s1 = j * HG_SUB, (j + 1) * HG_SUB
            ref_row = cum[s1 - 1:s1, :]
            qj = q * jnp.exp(jnp.minimum(cum - ref_row, 0.0))
            kj = k[s0:s1, :] * jnp.exp(ref_row - cum[s0:s1, :])
            mt = _dot_tn(v[s0:s1, :].astype(BF16), kj.astype(BF16)) * ones_f
            oj = _dot_nt(qj.astype(BF16), mt.astype(BF16))
            o = o + jnp.where(row >= s1, oj, 0.0)
        for lag in range(HG_SUB):
            if lag == 0:
                ks, cs, vs = k, cum, v
            else:
                ks = pltpu.roll(k, lag, 0)
                cs = pltpu.roll(cum, lag, 0)
                vs = pltpu.roll(v, lag, 0)
            term = q * ks * jnp.exp(jnp.minimum(cum - cs, 0.0))
            ssum = _dot(term.astype(BF16), ones_b)
            o = o + jnp.where(row % HG_SUB >= lag, ssum * vs, 0.0)
        last = cum[c - 1:c, :]
        kc = k * jnp.exp(last - cum)
        st_s[...] = st * jnp.exp(last) + _dot_tn(v.astype(BF16), kc.astype(BF16)) * ones_f
        y_ref[sl, :] = _hgrn_out(o, hg_ref[sl, :], g_ref[...], ones_b)
    st_ref[...] = st_s[...]


def _split_dot_left(w_bf16, x):
    hi = x.astype(BF16)
    lo = (x - hi.astype(F32)).astype(BF16)
    return _dot(w_bf16, hi) + _dot(w_bf16, lo)


def _hgrn_prompt(u3, lb, g_tiled, tt=256):
    bsz, t, _ = u3.shape
    full = lambda shape: pl.BlockSpec(shape, lambda b, i: (0,) * len(shape))
    col = lambda c: pl.BlockSpec((None, tt, MIX_W), lambda b, i: (b, i, c))
    return pl.pallas_call(
        _hgrn_prompt_kernel,
        grid=(bsz, t // tt),
        in_specs=[col(COL_HQ), col(COL_HF), col(COL_HI), col(COL_HG), full((1, MIX_W)), full((1, MIX_W))],
        out_specs=[pl.BlockSpec((None, tt, MIX_W), lambda b, i: (b, i, 0)),
                   pl.BlockSpec((None, MIX_W, MIX_W), lambda b, i: (b, 0, 0))],
        out_shape=[jax.ShapeDtypeStruct((bsz, t, MIX_W), F32),
                   jax.ShapeDtypeStruct((bsz, MIX_W, MIX_W), F32)],
        scratch_shapes=[pltpu.VMEM((MIX_W, MIX_W), F32)],
        compiler_params=_cparams("parallel", "arbitrary"),
    )(u3, u3, u3, u3, lb, g_tiled)


def _hgrn_decode_kernel(s0_ref, q_ref, f_ref, v_ref, lb_ref, sn_ref, o_ref):
    bb = s0_ref.shape[0]
    q, f, k = _hgrn_gates(q_ref[...], f_ref[...], lb_ref[...])
    sn = f * s0_ref[...] + k * v_ref[...]
    sn_ref[...] = sn
    o_ref[...] = jnp.sum((q * sn).reshape(bb, N_HEADS, HEAD_DIM, HEAD_DIM), axis=2)


def _hgrn_decode(s0, q_e, f_e, v_e, lb_e, bb=8):
    n = s0.shape[0]
    blk = pl.BlockSpec((bb, MIX_W, HEAD_DIM), lambda i: (i, 0, 0))
    return pl.pallas_call(
        _hgrn_decode_kernel,
        grid=(n // bb,),
        in_specs=[blk, blk, blk, blk, pl.BlockSpec((1, MIX_W, HEAD_DIM), lambda i: (0, 0, 0))],
        out_specs=[blk, pl.BlockSpec((bb, N_HEADS, HEAD_DIM), lambda i: (i, 0, 0))],
        out_shape=[jax.ShapeDtypeStruct((n, MIX_W, HEAD_DIM), F32),
                   jax.ShapeDtypeStruct((n, N_HEADS, HEAD_DIM), F32)],
        compiler_params=_cparams("parallel"),
    )(s0, q_e, f_e, v_e, lb_e)


def _hgrn_out_kernel(o_ref, hg_ref, g_ref, y_ref):
    y_ref[...] = _hgrn_out(o_ref[...], hg_ref[...], g_ref[...], _head_ones(MIX_W).astype(BF16))


def _hgrn_decode_out(o, u, g_tiled):
    n = o.shape[0]
    full = lambda shape: pl.BlockSpec(shape, lambda i: (0,) * len(shape))
    return pl.pallas_call(
        _hgrn_out_kernel,
        grid=(1,),
        in_specs=[full((n, MIX_W)), pl.BlockSpec((n, MIX_W), lambda i: (0, COL_HG)), full((1, MIX_W))],
        out_specs=full((n, MIX_W)),
        out_shape=jax.ShapeDtypeStruct((n, MIX_W), F32),
        compiler_params=_cparams("arbitrary"),
    )(o, u, g_tiled)


def _merge_kernel(ya_ref, yb_ref, yc_ref, yd_ref, x_ref, wg0_ref, wg1_ref, wg2_ref, wg3_ref, wb_ref, wo_ref,
                  lg_ref, lbias_ref, o_ref, *, alpha):
    x = x_ref[...]
    xb = x.astype(BF16)
    merged = None
    for n, (y_ref, wg_ref) in enumerate(zip((ya_ref, yb_ref, yc_ref, yd_ref), (wg0_ref, wg1_ref, wg2_ref, wg3_ref))):
        gate = _sigmoid(_dot(xb, wg_ref[...]))
        term = gate * _dot(y_ref[...].astype(BF16), wb_ref[n])
        merged = term if merged is None else merged + term
    out = _dot(merged.astype(BF16), wo_ref[...])
    o_ref[...] = _layer_norm(alpha * x + out, lg_ref[...], lbias_ref[...])


def _merge(ya, yb, yc, yd, x, w_in, wb, wo, ln_g, ln_b, alpha):
    n, d = x.shape
    tm = min(512, n)
    gate0 = N_MIX_COLS * MIX_W // d
    row = lambda width: pl.BlockSpec((tm, width), lambda i: (i, 0))
    full = lambda shape: pl.BlockSpec(shape, lambda i: (0,) * len(shape))
    gate = lambda b: pl.BlockSpec((d, d), lambda i, b=b: (0, gate0 + b))
    return pl.pallas_call(
        functools.partial(_merge_kernel, alpha=alpha),
        grid=(n // tm,),
        in_specs=[row(MIX_W), row(MIX_W), row(MIX_W), row(MIX_W), row(d),
                  gate(0), gate(1), gate(2), gate(3),
                  full(wb.shape), full(wo.shape), full((1, d)), full((1, d))],
        out_specs=row(d),
        out_shape=jax.ShapeDtypeStruct((n, d), F32),
        compiler_params=_cparams("parallel"),
    )(ya, yb, yc, yd, x, w_in, w_in, w_in, w_in, wb, wo, ln_g, ln_b)


def _router_weights(x, wr_hi, wr_lo, br):
    xh = x.astype(BF16)
    xl = (x - xh.astype(F32)).astype(BF16)
    logits = _dot(xh, wr_hi) + _dot(xh, wr_lo) + _dot(xl, wr_hi) + br
    lane = _iota((1, ROUTER_LANES), 1)
    big = jnp.int32(ROUTER_LANES)
    is_g = lane < N_GROUPS
    gl = jnp.where(is_g, logits, NEG_BIG)
    gmax = jnp.max(gl, axis=1, keepdims=True)
    g_idx = jnp.min(jnp.where(is_g & (gl == gmax), lane, big), axis=1, keepdims=True)
    g_w = 1.0 / jnp.sum(jnp.where(is_g, jnp.exp(gl - gmax), 0.0), axis=1, keepdims=True)
    in_grp = (lane >= ROUTER_E0) & (lane < ROUTER_E0 + N_EXPERTS) & \
             ((lane - ROUTER_E0) // EXPERTS_PER_GROUP == g_idx)
    el = jnp.where(in_grp, logits, NEG_BIG)
    v1 = jnp.max(el, axis=1, keepdims=True)
    i1 = jnp.min(jnp.where(in_grp & (el == v1), lane, big), axis=1, keepdims=True)
    el2 = jnp.where(lane == i1, NEG_BIG, el)
    v2 = jnp.max(el2, axis=1, keepdims=True)
    i2 = jnp.min(jnp.where(in_grp & (lane != i1) & (el2 == v2), lane, big), axis=1, keepdims=True)
    e2 = jnp.exp(v2 - v1)
    w1 = g_w / (1.0 + e2)
    w2 = g_w * e2 / (1.0 + e2)
    return jnp.where(lane == i1, w1, 0.0) + jnp.where(lane == i2, w2, 0.0)


def _moe_kernel(x_ref, wrh_ref, wrl_ref, br_ref, w1_ref, w3_ref, w2_ref, lg_ref, lbias_ref, o_ref,
                comb_s, acc_s, xb_s, *, alpha):
    g = pl.program_id(1)
    ff = w1_ref.shape[2]
    tm = x_ref.shape[0]

    @pl.when(g == 0)
    def _():
        x = x_ref[...]
        comb_s[...] = _router_weights(x, wrh_ref[...], wrl_ref[...], br_ref[...])
        xb_s[...] = x.astype(BF16)
        acc_s[...] = jnp.zeros_like(acc_s)

    lane = _iota((1, ROUTER_LANES), 1)
    comb = comb_s[...]
    cexp = []
    for e in range(EXPERTS_PER_GROUP):
        c_e = jnp.sum(jnp.where(lane == ROUTER_E0 + g * EXPERTS_PER_GROUP + e, comb, 0.0), axis=1, keepdims=True)
        cexp.append(jnp.broadcast_to(c_e, (tm, ff)))
    xb = xb_s[...]
    experts = range(EXPERTS_PER_GROUP)
    h1 = jnp.concatenate([_dot(xb, w1_ref[e]) for e in experts], axis=1)
    h3 = jnp.concatenate([_dot(xb, w3_ref[e]) for e in experts], axis=1)
    hid = _silu(h1) * h3 * jnp.concatenate(cexp, axis=1)
    acc_s[...] += _dot(hid.astype(BF16), w2_ref[...].reshape(EXPERTS_PER_GROUP * ff, -1))

    @pl.when(g == pl.num_programs(1) - 1)
    def _():
        o_ref[...] = _layer_norm(alpha * x_ref[...] + acc_s[...], lg_ref[...], lbias_ref[...])


def _moe(x, wr_hi, wr_lo, br, w1, w3, w2, ln_g, ln_b, alpha):
    n, d = x.shape
    ne, _, ff = w1.shape
    tm = min(1024, n)
    epg = EXPERTS_PER_GROUP
    full = lambda shape: pl.BlockSpec(shape, lambda i, g: (0,) * len(shape))
    return pl.pallas_call(
        functools.partial(_moe_kernel, alpha=alpha),
        grid=(n // tm, ne // epg),
        in_specs=[pl.BlockSpec((tm, d), lambda i, g: (i, 0)),
                  full(wr_hi.shape), full(wr_lo.shape), full(br.shape),
                  pl.BlockSpec((epg, d, ff), lambda i, g: (g, 0, 0)),
                  pl.BlockSpec((epg, d, ff), lambda i, g: (g, 0, 0)),
                  pl.BlockSpec((epg, ff, d), lambda i, g: (g, 0, 0)),
                  full((1, d)), full((1, d))],
        out_specs=pl.BlockSpec((tm, d), lambda i, g: (i, 0)),
        out_shape=jax.ShapeDtypeStruct((n, d), F32),
        scratch_shapes=[pltpu.VMEM((tm, ROUTER_LANES), F32), pltpu.VMEM((tm, d), F32),
                        pltpu.VMEM((tm, d), BF16)],
        compiler_params=_cparams("parallel", "arbitrary"),
    )(x, wr_hi, wr_lo, br, w1, w3, w2, ln_g, ln_b)


def _block_diag(w):
    nb, n, _ = w.shape
    eye = jnp.eye(nb, dtype=w.dtype)
    return (eye[:, None, :, None] * w[:, :, None, :]).reshape(nb * n, nb * n)


def _router_matrix(wg, bg, we, be):
    d = wg.shape[0]
    wr = jnp.zeros((d, ROUTER_LANES), F32)
    wr = wr.at[:, 0:N_GROUPS].set(wg).at[:, ROUTER_E0:ROUTER_E0 + N_EXPERTS].set(we)
    br = jnp.zeros((1, ROUTER_LANES), F32)
    br = br.at[0, 0:N_GROUPS].set(bg).at[0, ROUTER_E0:ROUTER_E0 + N_EXPERTS].set(be)
    hi = wr.astype(BF16)
    lo = (wr - hi.astype(F32)).astype(BF16)
    return hi, lo, br


def kernel(x_prompt, x_sample, cache_diff_k, cache_diff_v, cache_sb_k, cache_sb_v, page_table, state_conv, state_lru, state_hgrn, w_in, conv_w, conv_b, lru_wa, lru_ba, lru_wx, lru_bx, lru_lambda, diff_lam_q1, diff_lam_k1, diff_lam_q2, diff_lam_k2, diff_norm_g, hgrn_lb_raw, hgrn_norm_g, w_branch, w_out, ln1_g, ln1_b, router_group_w, router_group_b, router_expert_w, router_expert_b, exp_w1, exp_w3, exp_w2, ln2_g, ln2_b):
    depth = w_in.shape[0]
    bsz, seq, d = x_prompt.shape
    nd = x_sample.shape[0]
    n_pool = cache_diff_k.shape[1]
    alpha = (2 * depth) ** 0.25
    row = lambda a: a.reshape(1, -1)

    hg_lb = _hgrn_lower_bounds(hgrn_lb_raw)
    caches = [jnp.transpose(c, (0, 1, 3, 4, 2)).reshape(depth, n_pool, MIX_W, PAGE_SIZE)
              for c in (cache_diff_k, cache_diff_v, cache_sb_k, cache_sb_v)]

    xp = x_prompt.reshape(bsz * seq, d)
    xs = x_sample.reshape(nd, d)
    outs_p = [[] for _ in range(7)]
    outs_s = [[] for _ in range(7)]
    for l in range(depth):
        w_in_b = w_in[l].astype(BF16)
        wa = _block_diag(lru_wa[l]).astype(BF16)
        wx = _block_diag(lru_wx[l]).astype(BF16)
        lru = (conv_w[l], row(conv_b[l]), wa, row(lru_ba[l]), wx, row(lru_bx[l]), row(lru_lambda[l]))
        lam_vecs = jnp.stack([diff_lam_q1[l], diff_lam_k1[l], diff_lam_q2[l], diff_lam_k2[l]])
        consts = jnp.zeros((1, 128), F32).at[0, 0].set(0.8 - 0.6 * math.exp(-0.3 * l))
        dn_g = row(diff_norm_g[l])
        dn_g_t = jnp.tile(dn_g, (1, N_HEADS))
        hg_g_t = jnp.tile(row(hgrn_norm_g[l]), (1, N_HEADS))
        lb = row(hg_lb[l])
        wb = w_branch[l].astype(BF16)
        wo = w_out[l].astype(BF16)
        wr_hi, wr_lo, br = _router_matrix(router_group_w[l], router_group_b[l], router_expert_w[l], router_expert_b[l])
        w1, w3, w2 = exp_w1[l].astype(BF16), exp_w3[l].astype(BF16), exp_w2[l].astype(BF16)
        merge_w = (w_in_b, wb, wo, row(ln1_g[l]), row(ln1_b[l]))
        moe_w = (wr_hi, wr_lo, br, w1, w3, w2, row(ln2_g[l]), row(ln2_b[l]))

        u = _in_proj(xp, w_in_b)
        u3 = u.reshape(bsz, seq, -1)
        ya, conv_p, lru_p = _rglru_prompt(u3, *lru)
        yb = _diff_prompt(u3, lam_vecs, consts, dn_g)
        yc = _sb_prompt(u3)
        yd, st_t = _hgrn_prompt(u3, lb, hg_g_t)
        flat = lambda a: a.reshape(bsz * seq, MIX_W)
        x1 = _merge(flat(ya), flat(yb), flat(yc), flat(yd), xp, *merge_w, alpha)
        xp = _moe(x1, *moe_w, alpha)
        heads = lambda c: u[:, c * MIX_W:(c + 1) * MIX_W]
        st = jnp.stack([st_t[:, h * HEAD_DIM:(h + 1) * HEAD_DIM, h * HEAD_DIM:(h + 1) * HEAD_DIM]
                        for h in range(N_HEADS)], axis=1).swapaxes(-1, -2)
        for lst, val in zip(outs_p, (heads(COL_DK), heads(COL_DV), heads(COL_SK), heads(COL_SV), conv_p,
                                     lru_p.reshape(bsz, MIX_W), st)):
            lst.append(val)

        us = _in_proj(xs, w_in_b)
        ya, conv_s, lru_s = _rglru_decode(us, state_conv[l].reshape(nd, -1), state_lru[l], *lru)
        yb, yc = _decode_attn(l, page_table, caches, us.reshape(nd, 1, -1), lam_vecs, consts, dn_g_t)
        colb = lambda c: jnp.broadcast_to(us[:, c * MIX_W:(c + 1) * MIX_W, None], (nd, MIX_W, HEAD_DIM))
        v_e = jnp.broadcast_to(us[:, COL_HI * MIX_W:(COL_HI + 1) * MIX_W].reshape(nd, N_HEADS, 1, HEAD_DIM),
                               (nd, N_HEADS, HEAD_DIM, HEAD_DIM)).reshape(nd, MIX_W, HEAD_DIM)
        lb_e = jnp.broadcast_to(hg_lb[l][None, :, None], (1, MIX_W, HEAD_DIM))
        hgrn_s, o_d = _hgrn_decode(state_hgrn[l].reshape(nd, MIX_W, HEAD_DIM), colb(COL_HQ), colb(COL_HF), v_e, lb_e)
        yd = _hgrn_decode_out(o_d.reshape(nd, MIX_W), us, hg_g_t)
        x1 = _merge(ya, yb.reshape(nd, MIX_W), yc.reshape(nd, MIX_W), yd, xs, *merge_w, alpha)
        xs = _moe(x1, *moe_w, alpha)
        heads = lambda c: us[:, c * MIX_W:(c + 1) * MIX_W].reshape(nd, 1, N_HEADS, HEAD_DIM)
        for lst, val in zip(outs_s, (heads(COL_DK), heads(COL_DV), heads(COL_SK), heads(COL_SV),
                                     conv_s.reshape(nd, CONV_WIDTH - 1, MIX_W), lru_s,
                                     hgrn_s.reshape(nd, N_HEADS, HEAD_DIM, HEAD_DIM))):
            lst.append(val)

    stack = lambda lsts: [jnp.stack(v, axis=0) for v in lsts]
    outs_p = stack(outs_p)
    outs_p[:4] = [a.reshape(depth, bsz, seq, N_HEADS, HEAD_DIM) for a in outs_p[:4]]
    return (xp.reshape(bsz, seq, d), xs.reshape(nd, 1, d), *outs_p, *stack(outs_s))
```

```python
import functools
import math

import jax
import jax.numpy as jnp
from jax import lax
from jax.experimental import pallas as pl
from jax.experimental.pallas import tpu as pltpu

F32 = jnp.float32
BF16 = jnp.bfloat16

N_BRANCH = 4
MIX_W = 256
N_HEADS = 4
HEAD_DIM = 64
DIFF_QK_DIM = 32
CONV_WIDTH = 4
LRU_C = 8.0
N_GROUPS = 4
EXPERTS_PER_GROUP = 4
N_EXPERTS = 16
LN_EPS = 1e-5
RMS_EPS = 1e-5
NEG_BIG = -1e30
F_FLOOR = 1e-30
PAGE_SIZE = 128
LOG2E = math.log2(math.e)
LANES = 128

COL_XA, COL_GA, COL_DQ, COL_DK, COL_DV, COL_SQ, COL_SK, COL_SV, COL_HQ, COL_HF, COL_HI, COL_HG = range(12)
N_MIX_COLS = 12

V7X_VMEM_BYTES = 64 * 1024 * 1024
VMEM_LIMIT = V7X_VMEM_BYTES - 12 * 1024 * 1024

HG_CHUNK = 64
HG_SUB = 16
ROUTER_LANES = 128
ROUTER_E0 = 16
ROW_CHUNK = 32


def _cparams(*sem):
    return pltpu.CompilerParams(dimension_semantics=sem, vmem_limit_bytes=VMEM_LIMIT)


def _dot(a, b):
    return jnp.dot(a, b, preferred_element_type=F32)


def _dot_nt(a, b):
    return lax.dot_general(a, b, (((1,), (1,)), ((), ())), preferred_element_type=F32)


def _dot_tn(a, b):
    return lax.dot_general(a, b, (((0,), (0,)), ((), ())), preferred_element_type=F32)


def _split_dot(x, w_bf16):
    hi = x.astype(BF16)
    lo = (x - hi.astype(F32)).astype(BF16)
    return _dot(hi, w_bf16) + _dot(lo, w_bf16)


def _sigmoid(x):
    return 1.0 / (1.0 + jnp.exp(-x))


def _silu(x):
    return x * _sigmoid(x)


def _gelu_tanh(x):
    c = math.sqrt(2.0 / math.pi)
    return 0.5 * x * (1.0 + jnp.tanh(c * (x + 0.044715 * (x * x * x))))


def _softplus(x):
    return jnp.maximum(x, 0.0) + jnp.log(1.0 + jnp.exp(-jnp.abs(x)))


def _iota(shape, dim):
    return lax.broadcasted_iota(jnp.int32, shape, dim)


def _head_ones(n):
    return jnp.where((_iota((n, n), 0) // HEAD_DIM) == (_iota((n, n), 1) // HEAD_DIM), 1.0, 0.0)


def _layer_norm(h, g, b):
    mu = jnp.mean(h, axis=-1, keepdims=True)
    d = h - mu
    var = jnp.mean(d * d, axis=-1, keepdims=True)
    return d * lax.rsqrt(var + LN_EPS) * g + b


def _lb_kernel(raw_ref, o_ref):
    raw = raw_ref[...]
    m = jnp.max(raw, axis=0, keepdims=True)
    e = jnp.exp(raw - m)
    soft = e / jnp.sum(e, axis=0, keepdims=True)
    rows, run = [], jnp.zeros_like(soft[0:1, :])
    for l in range(raw.shape[0]):
        run = run + soft[l:l + 1, :]
        rows.append(run)
    cum = jnp.concatenate(rows, axis=0)
    o_ref[...] = jnp.clip(cum - soft[0:1, :], 0.0, 1.0)


def _hgrn_lower_bounds(raw):
    return pl.pallas_call(_lb_kernel, out_shape=jax.ShapeDtypeStruct(raw.shape, F32))(raw)


def _mm_kernel(x_ref, w_ref, o_ref):
    o_ref[...] = _dot(x_ref[...].astype(BF16), w_ref[...])


def _in_proj(x, w):
    n, k = x.shape
    c = N_MIX_COLS * MIX_W
    tm = min(1024, n)
    tn = 1024
    return pl.pallas_call(
        _mm_kernel,
        grid=(n // tm, c // tn),
        in_specs=[pl.BlockSpec((tm, k), lambda i, j: (i, 0)),
                  pl.BlockSpec((k, tn), lambda i, j: (0, j))],
        out_specs=pl.BlockSpec((tm, tn), lambda i, j: (i, j)),
        out_shape=jax.ShapeDtypeStruct((n, c), F32),
        compiler_params=_cparams("parallel", "parallel"),
    )(x, w)


KV_COLS = (COL_DK, COL_DV, COL_SK, COL_SV)


def _mm_kv_kernel(x_ref, w_ref, *refs, tn):
    o_ref = refs[len(KV_COLS)]
    kv_refs = refs[len(KV_COLS) + 1:]
    j = pl.program_id(1)
    res = _dot(x_ref[...].astype(BF16), w_ref[...])
    o_ref[...] = res
    for kv_ref, col in zip(kv_refs, KV_COLS):
        tile, local = divmod(col, tn // MIX_W)

        @pl.when(j == tile)
        def _(kv_ref=kv_ref, local=local):
            kv_ref[...] = res[:, local * MIX_W:(local + 1) * MIX_W]


def _in_proj_kv(x, w, layer, kv_stacks):
    n, k = x.shape
    c = N_MIX_COLS * MIX_W
    tm = min(1024, n)
    tn = 1024
    kv_spec = pl.BlockSpec((None, tm, MIX_W), lambda i, j: (layer, i, 0))
    n_kv = len(KV_COLS)
    return pl.pallas_call(
        functools.partial(_mm_kv_kernel, tn=tn),
        grid=(n // tm, c // tn),
        in_specs=[pl.BlockSpec((tm, k), lambda i, j: (i, 0)),
                  pl.BlockSpec((k, tn), lambda i, j: (0, j))] + [pl.BlockSpec(memory_space=pl.ANY)] * n_kv,
        out_specs=[pl.BlockSpec((tm, tn), lambda i, j: (i, j))] + [kv_spec] * n_kv,
        out_shape=[jax.ShapeDtypeStruct((n, c), F32)] + [jax.ShapeDtypeStruct(s.shape, F32) for s in kv_stacks],
        input_output_aliases={2 + a: 1 + a for a in range(n_kv)},
        compiler_params=_cparams("parallel", "arbitrary"),
    )(x, w, *kv_stacks)


def _lru_gates(xc, wa, ba, wx, bx, lam):
    xcb = xc.astype(BF16)
    r = _sigmoid(_dot(xcb, wa) + ba)
    i_g = _sigmoid(_dot(xcb, wx) + bx)
    log_a = -LRU_C * r * _softplus(-lam)
    a = jnp.exp(log_a)
    mult = jnp.sqrt(jnp.maximum(1.0 - jnp.exp(2.0 * log_a), 0.0))
    return a, mult, i_g


def _rglru_prompt_kernel(xa_ref, ga_ref, cw_ref, cb_ref, wa_ref, ba_ref, wx_ref, bx_ref, lam_ref,
                         y_ref, conv_ref, h_ref, xbuf, sa, sb, hc):
    t = pl.program_id(1)
    tt = xa_ref.shape[0]
    pad = tt // 2

    @pl.when(t == 0)
    def _():
        xbuf[0:8, :] = jnp.zeros((8, MIX_W), F32)
        hc[...] = jnp.zeros_like(hc)

    sa[0:pad, :] = jnp.ones((pad, MIX_W), F32)
    sb[0:pad, :] = jnp.zeros((pad, MIX_W), F32)

    xa = xa_ref[...]
    xbuf[8:8 + tt, :] = xa
    xc = cb_ref[...] + cw_ref[CONV_WIDTH - 1:CONV_WIDTH, :] * xa
    for i in range(CONV_WIDTH - 1):
        xc = xc + cw_ref[i:i + 1, :] * xbuf[5 + i:5 + i + tt, :]
    a, mult, i_g = _lru_gates(xc, wa_ref[...], ba_ref[...], wx_ref[...], bx_ref[...], lam_ref[...])
    pos = _iota((tt, 1), 0) + t * tt
    mult = jnp.where(pos == 0, 1.0, mult)
    b = mult * i_g * xc
    sa[pad:pad + tt, :] = a
    sb[pad:pad + tt, :] = b
    sb[pad:pad + 1, :] = b[0:1, :] + a[0:1, :] * hc[...]

    d = 1
    while d < tt:
        a_cur = sa[pad:pad + tt, :]
        b_cur = sb[pad:pad + tt, :]
        a_sh = sa[pad - d:pad - d + tt, :]
        b_sh = sb[pad - d:pad - d + tt, :]
        sb[pad:pad + tt, :] = a_cur * b_sh + b_cur
        if 2 * d < tt:
            sa[pad:pad + tt, :] = a_cur * a_sh
        d *= 2

    h = sb[pad:pad + tt, :]
    y_ref[...] = _gelu_tanh(ga_ref[...]) * h
    hc[...] = h[tt - 1:tt, :]
    xbuf[0:8, :] = xa[tt - 8:tt, :]
    conv_ref[...] = xa[tt - (CONV_WIDTH - 1):tt, :]
    h_ref[...] = h[tt - 1:tt, :]


def _rglru_prompt(u3, cw, cb, wa, ba, wx, bx, lam, tt=512):
    bsz, t, _ = u3.shape
    full = lambda shape: pl.BlockSpec(shape, lambda b, i: (0,) * len(shape))
    return pl.pallas_call(
        _rglru_prompt_kernel,
        grid=(bsz, t // tt),
        in_specs=[pl.BlockSpec((None, tt, MIX_W), lambda b, i: (b, i, COL_XA)),
                  pl.BlockSpec((None, tt, MIX_W), lambda b, i: (b, i, COL_GA)),
                  full((CONV_WIDTH, MIX_W)), full((1, MIX_W)), full((MIX_W, MIX_W)), full((1, MIX_W)),
                  full((MIX_W, MIX_W)), full((1, MIX_W)), full((1, MIX_W))],
        out_specs=[pl.BlockSpec((None, tt, MIX_W), lambda b, i: (b, i, 0)),
                   pl.BlockSpec((None, CONV_WIDTH - 1, MIX_W), lambda b, i: (b, 0, 0)),
                   pl.BlockSpec((None, 1, MIX_W), lambda b, i: (b, 0, 0))],
        out_shape=[jax.ShapeDtypeStruct((bsz, t, MIX_W), F32),
                   jax.ShapeDtypeStruct((bsz, CONV_WIDTH - 1, MIX_W), F32),
                   jax.ShapeDtypeStruct((bsz, 1, MIX_W), F32)],
        scratch_shapes=[pltpu.VMEM((tt + 8, MIX_W), F32),
                        pltpu.VMEM((tt + tt // 2, MIX_W), F32),
                        pltpu.VMEM((tt + tt // 2, MIX_W), F32),
                        pltpu.VMEM((1, MIX_W), F32)],
        compiler_params=_cparams("parallel", "arbitrary"),
    )(u3, u3, cw, cb, wa, ba, wx, bx, lam)


def _rglru_decode_kernel(xa_ref, ga_ref, conv_ref, h0_ref, cw_ref, cb_ref, wa_ref, ba_ref, wx_ref, bx_ref,
                         lam_ref, y_ref, convn_ref, h_ref):
    xa = xa_ref[...]
    w = MIX_W
    xc = cb_ref[...] + cw_ref[CONV_WIDTH - 1:CONV_WIDTH, :] * xa
    for i in range(CONV_WIDTH - 1):
        xc = xc + cw_ref[i:i + 1, :] * conv_ref[:, i * w:(i + 1) * w]
    a, mult, i_g = _lru_gates(xc, wa_ref[...], ba_ref[...], wx_ref[...], bx_ref[...], lam_ref[...])
    h = a * h0_ref[...] + mult * i_g * xc
    y_ref[...] = _gelu_tanh(ga_ref[...]) * h
    h_ref[...] = h
    convn_ref[:, 0:(CONV_WIDTH - 2) * w] = conv_ref[:, w:(CONV_WIDTH - 1) * w]
    convn_ref[:, (CONV_WIDTH - 2) * w:(CONV_WIDTH - 1) * w] = xa


def _rglru_decode(u, conv, h0, cw, cb, wa, ba, wx, bx, lam):
    n = u.shape[0]
    cwid = (CONV_WIDTH - 1) * MIX_W
    full = lambda shape: pl.BlockSpec(shape, lambda i: (0,) * len(shape))
    return pl.pallas_call(
        _rglru_decode_kernel,
        grid=(1,),
        in_specs=[pl.BlockSpec((n, MIX_W), lambda i: (0, COL_XA)),
                  pl.BlockSpec((n, MIX_W), lambda i: (0, COL_GA)),
                  full((n, cwid)), full((n, MIX_W)),
                  full((CONV_WIDTH, MIX_W)), full((1, MIX_W)), full((MIX_W, MIX_W)), full((1, MIX_W)),
                  full((MIX_W, MIX_W)), full((1, MIX_W)), full((1, MIX_W))],
        out_specs=[full((n, MIX_W)), full((n, cwid)), full((n, MIX_W))],
        out_shape=[jax.ShapeDtypeStruct((n, MIX_W), F32),
                   jax.ShapeDtypeStruct((n, cwid), F32),
                   jax.ShapeDtypeStruct((n, MIX_W), F32)],
        compiler_params=_cparams("arbitrary"),
    )(u, u, conv, h0, cw, cb, wa, ba, wx, bx, lam)


def _alibi_slope(h):
    return 2.0 ** (-8.0 * (h + 1) / N_HEADS)


def _diff_lambda(lam_ref, cst_ref):
    lv = lam_ref[...]
    s1 = jnp.sum(lv[0:1, :] * lv[1:2, :], axis=1, keepdims=True)
    s2 = jnp.sum(lv[2:3, :] * lv[3:4, :], axis=1, keepdims=True)
    lam_init = cst_ref[0:1, 0:1]
    return jnp.exp(s1) - jnp.exp(s2) + lam_init, lam_init


def _head_slab(h):
    return slice((h // 2) * LANES, (h // 2 + 1) * LANES), (h % 2) * HEAD_DIM


def _diff_q_rows(q, h):
    lane = _iota((1, LANES), 1)
    slab, lo = _head_slab(h)
    qs = q[:, slab] * (DIFF_QK_DIM ** -0.5 * LOG2E)
    q1 = jnp.where((lane >= lo) & (lane < lo + DIFF_QK_DIM), qs, 0.0)
    q2 = jnp.where((lane >= lo + DIFF_QK_DIM) & (lane < lo + HEAD_DIM), qs, 0.0)
    return jnp.concatenate([q1, q2], axis=0).astype(BF16)


def _diff_sum_lane(h):
    return HEAD_DIM if h % 2 == 0 else 0


def _diff_prompt_kernel(q_ref, k_ref, v_ref, lam_ref, cst_ref, g_ref, y_ref, kb_s, va_s, qs_s, m_s, acc_s):
    i = pl.program_id(1)
    tq = q_ref.shape[0]
    tk = tq
    t_all = k_ref.shape[0]
    rows = 2 * tq
    nl = tk // LANES

    @pl.when(i == 0)
    def _():
        lane = _iota((1, LANES), 1)

        def prep(c, carry):
            r0 = pl.multiple_of(c * tk, tk)
            kb_s[pl.ds(r0, tk), :] = k_ref[pl.ds(r0, tk), :].astype(BF16)
            v = v_ref[pl.ds(r0, tk), :]
            for h in range(N_HEADS):
                slab = v[:, (h // 2) * LANES:(h // 2 + 1) * LANES]
                va_s[h, pl.ds(r0, tk), :] = jnp.where(lane == _diff_sum_lane(h), 1.0, slab).astype(BF16)
            return carry

        lax.fori_loop(0, t_all // tk, prep, 0)

    q = q_ref[...]
    for h in range(N_HEADS):
        qs_s[h] = _diff_q_rows(q, h)
    m_s[...] = jnp.full(m_s.shape, NEG_BIG, F32)
    acc_s[...] = jnp.zeros_like(acc_s)
    row_in_tile = _iota((rows, LANES), 0) % tq
    col = _iota((rows, LANES), 1)

    def block(j, masked):
        k0 = pl.multiple_of(j * tk, tk)
        kpos = (_iota((1, tk), 1) + ((j - i) * tk - (tq - 1))).astype(F32)
        for h in range(N_HEADS):
            slab, _ = _head_slab(h)
            s = _dot_nt(qs_s[h], kb_s[pl.ds(k0, tk), slab])
            bias = kpos * (_alibi_slope(h) * LOG2E)
            cols = []
            for c in range(nl):
                sc = s[:, c * LANES:(c + 1) * LANES] + bias[:, c * LANES:(c + 1) * LANES]
                if masked:
                    sc = jnp.where(col + c * LANES <= row_in_tile, sc, NEG_BIG)
                cols.append(sc)
            mx = cols[0]
            for sc in cols[1:]:
                mx = jnp.maximum(mx, sc)
            m_prev = m_s[h]
            m_new = jnp.maximum(m_prev, jnp.max(mx, axis=1, keepdims=True))
            p = jnp.concatenate([jnp.exp2(sc - m_new) for sc in cols], axis=1).astype(BF16)
            acc_s[h] = jnp.exp2(m_prev - m_new) * acc_s[h] + _dot(p, va_s[h, pl.ds(k0, tk), :])
            m_s[h] = m_new

    def body(j, carry):
        block(j, False)
        return carry

    lax.fori_loop(0, i, body, 0)
    block(i, True)
    lam, lam_init = _diff_lambda(lam_ref, cst_ref)
    outs = []
    for h in range(N_HEADS):
        acc = acc_s[h]
        lo = (h % 2) * HEAD_DIM
        sl = _diff_sum_lane(h)
        o = acc[:, lo:lo + HEAD_DIM] / acc[:, sl:sl + 1]
        o = o[0:tq, :] - lam * o[tq:rows, :]
        inv = lax.rsqrt(jnp.mean(o * o, axis=1, keepdims=True) + RMS_EPS)
        outs.append(o * inv * g_ref[...] * (1.0 - lam_init))
    y_ref[...] = jnp.concatenate(outs, axis=1)


def _diff_prompt(u3, lam_vecs, consts, norm_g, tq=512):
    bsz, t, _ = u3.shape
    full = lambda shape: pl.BlockSpec(shape, lambda b, i: (0,) * len(shape))
    return pl.pallas_call(
        _diff_prompt_kernel,
        grid=(bsz, t // tq),
        in_specs=[pl.BlockSpec((None, tq, MIX_W), lambda b, i: (b, i, COL_DQ)),
                  pl.BlockSpec((None, t, MIX_W), lambda b, i: (b, 0, COL_DK)),
                  pl.BlockSpec((None, t, MIX_W), lambda b, i: (b, 0, COL_DV)),
                  full(lam_vecs.shape), full(consts.shape), full(norm_g.shape)],
        out_specs=pl.BlockSpec((None, tq, MIX_W), lambda b, i: (b, i, 0)),
        out_shape=jax.ShapeDtypeStruct((bsz, t, MIX_W), F32),
        scratch_shapes=[pltpu.VMEM((t, MIX_W), BF16), pltpu.VMEM((N_HEADS, t, LANES), BF16),
                        pltpu.VMEM((N_HEADS, 2 * tq, LANES), BF16),
                        pltpu.VMEM((N_HEADS, 2 * tq, LANES), F32), pltpu.VMEM((N_HEADS, 2 * tq, LANES), F32)],
        compiler_params=_cparams("parallel", "arbitrary"),
    )(u3, u3, u3, lam_vecs, consts, norm_g)


def _neg_log2_keep(z2):
    e = jnp.exp2(jnp.minimum(z2, -z2))
    return jnp.maximum(z2, 0.0) + jnp.log2(1.0 + e)


def _strict_upper(n):
    return jnp.where(_iota((n, n), 0) > _iota((n, n), 1), 1.0, 0.0).astype(BF16)


def _sb_prompt_kernel(q_ref, k_ref, v_ref, y_ref, kb_s, vb_s, qs_s, r_s, acc_s, hl_s, tot_s, w_s, *, tk):
    i = pl.program_id(1)
    tq = q_ref.shape[0]
    t_all = k_ref.shape[0]
    nl = tk // LANES
    per_tile = tq // tk

    @pl.when(i == 0)
    def _():
        def prep(c, carry):
            r0 = pl.multiple_of(c * tq, tq)
            kb_s[pl.ds(r0, tq), :] = k_ref[pl.ds(r0, tq), :].astype(BF16)
            vb_s[pl.ds(r0, tq), :] = v_ref[pl.ds(r0, tq), :].astype(BF16)
            return carry

        lax.fori_loop(0, t_all // tq, prep, 0)

    q = q_ref[...]
    lane = _iota((1, LANES), 1)
    for h in range(N_HEADS):
        slab, lo = _head_slab(h)
        qs_s[h] = jnp.where((lane >= lo) & (lane < lo + HEAD_DIM), q[:, slab] * (HEAD_DIM ** -0.5 * LOG2E),
                            0.0).astype(BF16)
    r_s[...] = jnp.zeros_like(r_s)
    acc_s[...] = jnp.zeros_like(acc_s)
    col = _iota((ROW_CHUNK, LANES), 1)
    row0 = _iota((ROW_CHUNK, LANES), 0)
    tri = jnp.where(_iota((2 * tk, tk), 0) % tk >= _iota((2 * tk, tk), 1), 1.0, 0.0).astype(BF16)
    chunks = [slice(r, r + ROW_CHUNK) for r in range(0, tq, ROW_CHUNK)]

    def block(j, key_off):
        k0 = pl.multiple_of(j * tk, tk)

        def earlier(rs, c):
            return col + (c * LANES + key_off) < row0 + rs.start

        z2s = [_dot_nt(qs_s[h], kb_s[pl.ds(k0, tk), _head_slab(h)[0]]) for h in range(N_HEADS)]
        for h in range(N_HEADS):
            for rs in chunks:
                nk = _neg_log2_keep(z2s[h][rs, :])
                if key_off is not None:
                    nk = jnp.concatenate([jnp.where(earlier(rs, c), nk[:, c * LANES:(c + 1) * LANES], 0.0)
                                          for c in range(nl)], axis=1)
                hi = nk.astype(BF16)
                hl_s[h, rs, 0:tk] = hi
                hl_s[h, rs, tk:2 * tk] = (nk - hi.astype(F32)).astype(BF16)
                tot_s[h, rs, :] = jnp.broadcast_to(jnp.sum(nk, axis=1, keepdims=True), (ROW_CHUNK, LANES))
        incl = [_dot(hl_s[h], tri) for h in range(N_HEADS)]
        for h in range(N_HEADS):
            for rs in chunks:
                rr = r_s[h, rs, :]
                z2 = z2s[h][rs, :]
                inc = incl[h][rs, :]
                ws = []
                for c in range(nl):
                    sl = slice(c * LANES, (c + 1) * LANES)
                    wc = jnp.exp2(jnp.minimum(z2[:, sl] - inc[:, sl] - rr, 0.0))
                    if key_off is not None:
                        wc = jnp.where(earlier(rs, c), wc, 0.0)
                    ws.append(wc)
                w_s[h, rs, :] = jnp.concatenate(ws, axis=1).astype(BF16)
                r_s[h, rs, :] = rr + tot_s[h, rs, :]
            acc_s[h] = acc_s[h] + _dot(w_s[h], vb_s[pl.ds(k0, tk), _head_slab(h)[0]])

    for d in range(per_tile - 1, -1, -1):
        block(i * per_tile + d, d * tk)

    def body(jj, carry):
        block(i * per_tile - 1 - jj, None)
        return carry

    lax.fori_loop(0, i * per_tile, body, 0)
    y_ref[...] = jnp.concatenate([acc_s[h][:, _head_slab(h)[1]:_head_slab(h)[1] + HEAD_DIM]
                                  for h in range(N_HEADS)], axis=1)


def _sb_prompt(u3, tq=512, tk=256):
    bsz, t, _ = u3.shape
    return pl.pallas_call(
        functools.partial(_sb_prompt_kernel, tk=tk),
        grid=(bsz, t // tq),
        in_specs=[pl.BlockSpec((None, tq, MIX_W), lambda b, i: (b, i, COL_SQ)),
                  pl.BlockSpec((None, t, MIX_W), lambda b, i: (b, 0, COL_SK)),
                  pl.BlockSpec((None, t, MIX_W), lambda b, i: (b, 0, COL_SV))],
        out_specs=pl.BlockSpec((None, tq, MIX_W), lambda b, i: (b, i, 0)),
        out_shape=jax.ShapeDtypeStruct((bsz, t, MIX_W), F32),
        scratch_shapes=[pltpu.VMEM((t, MIX_W), BF16), pltpu.VMEM((t, MIX_W), BF16),
                        pltpu.VMEM((N_HEADS, tq, LANES), BF16),
                        pltpu.VMEM((N_HEADS, tq, LANES), F32), pltpu.VMEM((N_HEADS, tq, LANES), F32),
                        pltpu.VMEM((N_HEADS, tq, 2 * tk), BF16), pltpu.VMEM((N_HEADS, tq, LANES), F32),
                        pltpu.VMEM((N_HEADS, tq, tk), BF16)],
        compiler_params=_cparams("parallel", "arbitrary"),
    )(u3, u3, u3)


def _decode_attn_one(dk_refs, dv_refs, sk_refs, sv_refs, dq, dkn, dvn, sq, lam_ref, cst_ref, g_ref):
    past = len(dk_refs) * PAGE_SIZE
    w = MIX_W
    lane = _iota((1, w), 1)
    row8 = _iota((2 * N_HEADS, 1), 0)
    cat = lambda page_refs: jnp.concatenate([r[...].astype(BF16) for r in page_refs], axis=1)

    q = dq * (DIFF_QK_DIM ** -0.5 * LOG2E)
    seg = lane // DIFF_QK_DIM
    qrows = jnp.where(seg == row8, q, 0.0)
    slope2 = jnp.exp((row8 // 2 + 1).astype(F32) * (-8.0 * math.log(2.0) / N_HEADS)) * LOG2E
    kpos = _iota((1, past), 1).astype(F32)
    sc = _dot(qrows.astype(BF16), cat(dk_refs)) - slope2 * (float(past) - kpos)
    s_new = jnp.sum(qrows * dkn, axis=1, keepdims=True)
    m = jnp.maximum(s_new, jnp.max(sc, axis=1, keepdims=True))
    p_new = jnp.exp2(s_new - m)
    pr = jnp.exp2(sc - m)
    l = p_new + jnp.sum(pr, axis=1, keepdims=True)
    acc = p_new * dvn + _dot_nt(pr.astype(BF16), cat(dv_refs))
    o = acc / l
    lam, lam_init = _diff_lambda(lam_ref, cst_ref)
    coef = jnp.where(row8 % 2 == 0, 1.0, -lam)
    head_of_lane = lane // HEAD_DIM
    o = jnp.where(head_of_lane == row8 // 2, o * coef, 0.0)
    o = jnp.sum(o, axis=0, keepdims=True)
    ms = _split_dot(o * o, _head_ones(w).astype(BF16)) * (1.0 / HEAD_DIM)
    yb = o * lax.rsqrt(ms + RMS_EPS) * g_ref[...] * (1.0 - lam_init)

    rowh = _iota((N_HEADS, 1), 0)
    qsb = jnp.where(head_of_lane == rowh, sq * (HEAD_DIM ** -0.5 * LOG2E), 0.0).astype(BF16)
    z2 = _dot(qsb, cat(sk_refs))
    nk = _neg_log2_keep(z2)
    chunk = 2 * PAGE_SIZE
    upper = _strict_upper(chunk)
    run = jnp.zeros((N_HEADS, 1), F32)
    later = [None] * (past // chunk)
    for c in range(past // chunk - 1, -1, -1):
        nk_c = nk[:, c * chunk:(c + 1) * chunk]
        later[c] = _split_dot(nk_c, upper) + run
        run = run + jnp.sum(nk_c, axis=1, keepdims=True)
    wgt = jnp.exp2(jnp.minimum(z2 - nk - jnp.concatenate(later, axis=1), 0.0))
    acc = _dot_nt(wgt.astype(BF16), cat(sv_refs))
    yc = jnp.sum(jnp.where(head_of_lane == rowh, acc, 0.0), axis=0, keepdims=True)
    return yb, yc


N_CACHES = 4


def _decode_attn_kernel(pt_ref, dkc_ref, dvc_ref, skc_ref, svc_ref, dq_ref, dkn_ref, dvn_ref, sq_ref,
                        lam_ref, cst_ref, g_ref, yb_ref, yc_ref, page_buf, page_sem, *, layer, n_pages, n_seq):
    b = pl.program_id(0)
    slot = lax.rem(b, 2)
    caches = (dkc_ref, dvc_ref, skc_ref, svc_ref)

    def page_copy(step, slot_, c, si, p):
        page = pt_ref[step * n_seq + si, p]
        return pltpu.make_async_copy(caches[c].at[layer, page], page_buf.at[slot_, c, si * n_pages + p],
                                     page_sem.at[slot_])

    def all_pages(step, slot_):
        return [page_copy(step, slot_, c, si, p)
                for c in range(N_CACHES) for si in range(n_seq) for p in range(n_pages)]

    @pl.when(b == 0)
    def _():
        for cp in all_pages(0, 0):
            cp.start()

    @pl.when(b + 1 < pl.num_programs(0))
    def _():
        for cp in all_pages(b + 1, 1 - slot):
            cp.start()

    for cp in all_pages(b, slot):
        cp.wait()

    for si in range(n_seq):
        pages = [[page_buf.at[slot, c, si * n_pages + p] for p in range(n_pages)] for c in range(N_CACHES)]
        yb, yc = _decode_attn_one(*pages, dq_ref[si], dkn_ref[si], dvn_ref[si], sq_ref[si],
                                  lam_ref, cst_ref, g_ref)
        yb_ref[si] = yb
        yc_ref[si] = yc


def _decode_attn(layer, page_table, caches, u3, lam_vecs, consts, norm_g_tiled, n_seq=2):
    n, n_pages = page_table.shape

    def col_spec(c):
        return pl.BlockSpec((n_seq, 1, MIX_W), lambda b, pt: (b, 0, c))

    full = lambda shape: pl.BlockSpec(shape, lambda b, pt: (0,) * len(shape))
    in_specs = [pl.BlockSpec(memory_space=pl.ANY)] * N_CACHES
    in_specs += [col_spec(COL_DQ), col_spec(COL_DK), col_spec(COL_DV), col_spec(COL_SQ),
                 full(lam_vecs.shape), full(consts.shape), full(norm_g_tiled.shape)]
    out_spec = pl.BlockSpec((n_seq, 1, MIX_W), lambda b, pt: (b, 0, 0))
    return pl.pallas_call(
        functools.partial(_decode_attn_kernel, layer=layer, n_pages=n_pages, n_seq=n_seq),
        grid_spec=pltpu.PrefetchScalarGridSpec(
            num_scalar_prefetch=1, grid=(n // n_seq,), in_specs=in_specs, out_specs=[out_spec, out_spec],
            scratch_shapes=[pltpu.VMEM((2, N_CACHES, n_seq * n_pages, MIX_W, PAGE_SIZE), F32),
                            pltpu.SemaphoreType.DMA((2,))]),
        out_shape=[jax.ShapeDtypeStruct((n, 1, MIX_W), F32), jax.ShapeDtypeStruct((n, 1, MIX_W), F32)],
        compiler_params=_cparams("arbitrary"),
    )(page_table, *caches, u3, u3, u3, u3, lam_vecs, consts, norm_g_tiled)


def _hgrn_gates(hq, hf, lb):
    q = _silu(hq)
    e = jnp.exp(-jnp.abs(hf))
    inv = 1.0 / (1.0 + e)
    pos = hf >= 0.0
    sig = jnp.where(pos, inv, e * inv)
    sig_n = jnp.where(pos, e * inv, inv)
    f = lb + (1.0 - lb) * sig
    return q, f, (1.0 - lb) * sig_n


def _hgrn_out(o, hg, g, ones_bf16):
    ms = _split_dot(o * o, ones_bf16) * (1.0 / HEAD_DIM)
    return o * lax.rsqrt(ms + RMS_EPS) * g * _silu(hg)


def _hgrn_prompt_kernel(hq_ref, hf_ref, hi_ref, hg_ref, lb_ref, g_ref, y_ref, st_ref, st_s):
    t = pl.program_id(1)
    tt = hq_ref.shape[0]
    c = HG_CHUNK
    w = MIX_W

    @pl.when(t == 0)
    def _():
        st_s[...] = jnp.zeros_like(st_s)

    lb = lb_ref[...]
    ones_f = _head_ones(w)
    ones_b = ones_f.astype(BF16)
    tril = jnp.where(_iota((c, c), 1) <= _iota((c, c), 0), 1.0, 0.0).astype(BF16)
    row = _iota((c, 1), 0)
    for ci in range(tt // c):
        sl = slice(ci * c, (ci + 1) * c)
        q, f, k = _hgrn_gates(hq_ref[sl, :], hf_ref[sl, :], lb)
        v = hi_ref[sl, :]
        cum = _split_dot_left(tril, jnp.log(jnp.maximum(f, F_FLOOR)))
        st = st_s[...]
        o = _dot_nt((q * jnp.exp(cum)).astype(BF16), st.astype(BF16))
        for j in range(c // HG_SUB - 1):
            s0, s1 = j * HG_SUB, (j + 1) * HG_SUB
            ref_row = cum[s1 - 1:s1, :]
            qj = q * jnp.exp(jnp.minimum(cum - ref_row, 0.0))
            kj = k[s0:s1, :] * jnp.exp(ref_row - cum[s0:s1, :])
            mt = _dot_tn(v[s0:s1, :].astype(BF16), kj.astype(BF16)) * ones_f
            oj = _dot_nt(qj.astype(BF16), mt.astype(BF16))
            o = o + jnp.where(row >= s1, oj, 0.0)
        for lag in range(HG_SUB):
            if lag == 0:
                ks, cs, vs = k, cum, v
            else:
                ks = pltpu.roll(k, lag, 0)
                cs = pltpu.roll(cum, lag, 0)
                vs = pltpu.roll(v, lag, 0)
            term = q * ks * jnp.exp(jnp.minimum(cum - cs, 0.0))
            ssum = _dot(term.astype(BF16), ones_b)
            o = o + jnp.where(row % HG_SUB >= lag, ssum * vs, 0.0)
        last = cum[c - 1:c, :]
        kc = k * jnp.exp(last - cum)
        st_s[...] = st * jnp.exp(last) + _dot_tn(v.astype(BF16), kc.astype(BF16)) * ones_f
        y_ref[sl, :] = _hgrn_out(o, hg_ref[sl, :], g_ref[...], ones_b)
    st_ref[...] = st_s[...]


def _split_dot_left(w_bf16, x):
    hi = x.astype(BF16)
    lo = (x - hi.astype(F32)).astype(BF16)
    return _dot(w_bf16, hi) + _dot(w_bf16, lo)


def _hgrn_prompt(u3, lb, g_tiled, tt=256):
    bsz, t, _ = u3.shape
    full = lambda shape: pl.BlockSpec(shape, lambda b, i: (0,) * len(shape))
    col = lambda c: pl.BlockSpec((None, tt, MIX_W), lambda b, i: (b, i, c))
    return pl.pallas_call(
        _hgrn_prompt_kernel,
        grid=(bsz, t // tt),
        in_specs=[col(COL_HQ), col(COL_HF), col(COL_HI), col(COL_HG), full((1, MIX_W)), full((1, MIX_W))],
        out_specs=[pl.BlockSpec((None, tt, MIX_W), lambda b, i: (b, i, 0)),
                   pl.BlockSpec((None, MIX_W, MIX_W), lambda b, i: (b, 0, 0))],
        out_shape=[jax.ShapeDtypeStruct((bsz, t, MIX_W), F32),
                   jax.ShapeDtypeStruct((bsz, MIX_W, MIX_W), F32)],
        scratch_shapes=[pltpu.VMEM((MIX_W, MIX_W), F32)],
        compiler_params=_cparams("parallel", "arbitrary"),
    )(u3, u3, u3, u3, lb, g_tiled)


def _hgrn_decode_kernel(s0_ref, q_ref, f_ref, v_ref, lb_ref, sn_ref, o_ref):
    bb = s0_ref.shape[0]
    q, f, k = _hgrn_gates(q_ref[...], f_ref[...], lb_ref[...])
    sn = f * s0_ref[...] + k * v_ref[...]
    sn_ref[...] = sn
    o_ref[...] = jnp.sum((q * sn).reshape(bb, N_HEADS, HEAD_DIM, HEAD_DIM), axis=2)


def _hgrn_decode(s0, q_e, f_e, v_e, lb_e, bb=8):
    n = s0.shape[0]
    blk = pl.BlockSpec((bb, MIX_W, HEAD_DIM), lambda i: (i, 0, 0))
    return pl.pallas_call(
        _hgrn_decode_kernel,
        grid=(n // bb,),
        in_specs=[blk, blk, blk, blk, pl.BlockSpec((1, MIX_W, HEAD_DIM), lambda i: (0, 0, 0))],
        out_specs=[blk, pl.BlockSpec((bb, N_HEADS, HEAD_DIM), lambda i: (i, 0, 0))],
        out_shape=[jax.ShapeDtypeStruct((n, MIX_W, HEAD_DIM), F32),
                   jax.ShapeDtypeStruct((n, N_HEADS, HEAD_DIM), F32)],
        compiler_params=_cparams("parallel"),
    )(s0, q_e, f_e, v_e, lb_e)


def _hgrn_out_kernel(o_ref, hg_ref, g_ref, y_ref):
    y_ref[...] = _hgrn_out(o_ref[...], hg_ref[...], g_ref[...], _head_ones(MIX_W).astype(BF16))


def _hgrn_decode_out(o, u, g_tiled):
    n = o.shape[0]
    full = lambda shape: pl.BlockSpec(shape, lambda i: (0,) * len(shape))
    return pl.pallas_call(
        _hgrn_out_kernel,
        grid=(1,),
        in_specs=[full((n, MIX_W)), pl.BlockSpec((n, MIX_W), lambda i: (0, COL_HG)), full((1, MIX_W))],
        out_specs=full((n, MIX_W)),
        out_shape=jax.ShapeDtypeStruct((n, MIX_W), F32),
        compiler_params=_cparams("arbitrary"),
    )(o, u, g_tiled)


def _merge_kernel(ya_ref, yb_ref, yc_ref, yd_ref, x_ref, wg0_ref, wg1_ref, wg2_ref, wg3_ref, wb_ref, wo_ref,
                  lg_ref, lbias_ref, o_ref, *, alpha):
    x = x_ref[...]
    xb = x.astype(BF16)
    merged = None
    for n, (y_ref, wg_ref) in enumerate(zip((ya_ref, yb_ref, yc_ref, yd_ref), (wg0_ref, wg1_ref, wg2_ref, wg3_ref))):
        gate = _sigmoid(_dot(xb, wg_ref[...]))
        term = gate * _dot(y_ref[...].astype(BF16), wb_ref[n])
        merged = term if merged is None else merged + term
    out = _dot(merged.astype(BF16), wo_ref[...])
    o_ref[...] = _layer_norm(alpha * x + out, lg_ref[...], lbias_ref[...])


def _merge(ya, yb, yc, yd, x, w_in, wb, wo, ln_g, ln_b, alpha):
    n, d = x.shape
    tm = min(512, n)
    gate0 = N_MIX_COLS * MIX_W // d
    row = lambda width: pl.BlockSpec((tm, width), lambda i: (i, 0))
    full = lambda shape: pl.BlockSpec(shape, lambda i: (0,) * len(shape))
    gate = lambda b: pl.BlockSpec((d, d), lambda i, b=b: (0, gate0 + b))
    return pl.pallas_call(
        functools.partial(_merge_kernel, alpha=alpha),
        grid=(n // tm,),
        in_specs=[row(MIX_W), row(MIX_W), row(MIX_W), row(MIX_W), row(d),
                  gate(0), gate(1), gate(2), gate(3),
                  full(wb.shape), full(wo.shape), full((1, d)), full((1, d))],
        out_specs=row(d),
        out_shape=jax.ShapeDtypeStruct((n, d), F32),
        compiler_params=_cparams("parallel"),
    )(ya, yb, yc, yd, x, w_in, w_in, w_in, w_in, wb, wo, ln_g, ln_b)


def _router_weights(x, wr_hi, wr_lo, br):
    xh = x.astype(BF16)
    xl = (x - xh.astype(F32)).astype(BF16)
    logits = _dot(xh, wr_hi) + _dot(xh, wr_lo) + _dot(xl, wr_hi) + br
    lane = _iota((1, ROUTER_LANES), 1)
    big = jnp.int32(ROUTER_LANES)
    is_g = lane < N_GROUPS
    gl = jnp.where(is_g, logits, NEG_BIG)
    gmax = jnp.max(gl, axis=1, keepdims=True)
    g_idx = jnp.min(jnp.where(is_g & (gl == gmax), lane, big), axis=1, keepdims=True)
    g_w = 1.0 / jnp.sum(jnp.where(is_g, jnp.exp(gl - gmax), 0.0), axis=1, keepdims=True)
    in_grp = (lane >= ROUTER_E0) & (lane < ROUTER_E0 + N_EXPERTS) & \
             ((lane - ROUTER_E0) // EXPERTS_PER_GROUP == g_idx)
    el = jnp.where(in_grp, logits, NEG_BIG)
    v1 = jnp.max(el, axis=1, keepdims=True)
    i1 = jnp.min(jnp.where(in_grp & (el == v1), lane, big), axis=1, keepdims=True)
    el2 = jnp.where(lane == i1, NEG_BIG, el)
    v2 = jnp.max(el2, axis=1, keepdims=True)
    i2 = jnp.min(jnp.where(in_grp & (lane != i1) & (el2 == v2), lane, big), axis=1, keepdims=True)
    e2 = jnp.exp(v2 - v1)
    w1 = g_w / (1.0 + e2)
    w2 = g_w * e2 / (1.0 + e2)
    return jnp.where(lane == i1, w1, 0.0) + jnp.where(lane == i2, w2, 0.0)


def _moe_kernel(x_ref, wrh_ref, wrl_ref, br_ref, w1_ref, w3_ref, w2_ref, lg_ref, lbias_ref, o_ref,
                comb_s, acc_s, xb_s, *, alpha):
    g = pl.program_id(1)
    ff = w1_ref.shape[2]
    tm = x_ref.shape[0]

    @pl.when(g == 0)
    def _():
        x = x_ref[...]
        comb_s[...] = _router_weights(x, wrh_ref[...], wrl_ref[...], br_ref[...])
        xb_s[...] = x.astype(BF16)
        acc_s[...] = jnp.zeros_like(acc_s)

    lane = _iota((1, ROUTER_LANES), 1)
    comb = comb_s[...]
    cexp = []
    for e in range(EXPERTS_PER_GROUP):
        c_e = jnp.sum(jnp.where(lane == ROUTER_E0 + g * EXPERTS_PER_GROUP + e, comb, 0.0), axis=1, keepdims=True)
        cexp.append(jnp.broadcast_to(c_e, (tm, ff)))
    xb = xb_s[...]
    experts = range(EXPERTS_PER_GROUP)
    h1 = jnp.concatenate([_dot(xb, w1_ref[e]) for e in experts], axis=1)
    h3 = jnp.concatenate([_dot(xb, w3_ref[e]) for e in experts], axis=1)
    hid = _silu(h1) * h3 * jnp.concatenate(cexp, axis=1)
    acc_s[...] += _dot(hid.astype(BF16), w2_ref[...].reshape(EXPERTS_PER_GROUP * ff, -1))

    @pl.when(g == pl.num_programs(1) - 1)
    def _():
        o_ref[...] = _layer_norm(alpha * x_ref[...] + acc_s[...], lg_ref[...], lbias_ref[...])


def _moe(x, wr_hi, wr_lo, br, w1, w3, w2, ln_g, ln_b, alpha):
    n, d = x.shape
    ne, _, ff = w1.shape
    tm = min(1024, n)
    epg = EXPERTS_PER_GROUP
    full = lambda shape: pl.BlockSpec(shape, lambda i, g: (0,) * len(shape))
    return pl.pallas_call(
        functools.partial(_moe_kernel, alpha=alpha),
        grid=(n // tm, ne // epg),
        in_specs=[pl.BlockSpec((tm, d), lambda i, g: (i, 0)),
                  full(wr_hi.shape), full(wr_lo.shape), full(br.shape),
                  pl.BlockSpec((epg, d, ff), lambda i, g: (g, 0, 0)),
                  pl.BlockSpec((epg, d, ff), lambda i, g: (g, 0, 0)),
                  pl.BlockSpec((epg, ff, d), lambda i, g: (g, 0, 0)),
                  full((1, d)), full((1, d))],
        out_specs=pl.BlockSpec((tm, d), lambda i, g: (i, 0)),
        out_shape=jax.ShapeDtypeStruct((n, d), F32),
        scratch_shapes=[pltpu.VMEM((tm, ROUTER_LANES), F32), pltpu.VMEM((tm, d), F32),
                        pltpu.VMEM((tm, d), BF16)],
        compiler_params=_cparams("parallel", "arbitrary"),
    )(x, wr_hi, wr_lo, br, w1, w3, w2, ln_g, ln_b)


def _block_diag(w):
    nb, n, _ = w.shape
    eye = jnp.eye(nb, dtype=w.dtype)
    return (eye[:, None, :, None] * w[:, :, None, :]).reshape(nb * n, nb * n)


def _router_matrix(wg, bg, we, be):
    d = wg.shape[0]
    wr = jnp.zeros((d, ROUTER_LANES), F32)
    wr = wr.at[:, 0:N_GROUPS].set(wg).at[:, ROUTER_E0:ROUTER_E0 + N_EXPERTS].set(we)
    br = jnp.zeros((1, ROUTER_LANES), F32)
    br = br.at[0, 0:N_GROUPS].set(bg).at[0, ROUTER_E0:ROUTER_E0 + N_EXPERTS].set(be)
    hi = wr.astype(BF16)
    lo = (wr - hi.astype(F32)).astype(BF16)
    return hi, lo, br


def kernel(x_prompt, x_sample, cache_diff_k, cache_diff_v, cache_sb_k, cache_sb_v, page_table, state_conv, state_lru, state_hgrn, w_in, conv_w, conv_b, lru_wa, lru_ba, lru_wx, lru_bx, lru_lambda, diff_lam_q1, diff_lam_k1, diff_lam_q2, diff_lam_k2, diff_norm_g, hgrn_lb_raw, hgrn_norm_g, w_branch, w_out, ln1_g, ln1_b, router_group_w, router_group_b, router_expert_w, router_expert_b, exp_w1, exp_w3, exp_w2, ln2_g, ln2_b):
    depth = w_in.shape[0]
    bsz, seq, d = x_prompt.shape
    nd = x_sample.shape[0]
    n_pool = cache_diff_k.shape[1]
    alpha = (2 * depth) ** 0.25
    row = lambda a: a.reshape(1, -1)

    hg_lb = _hgrn_lower_bounds(hgrn_lb_raw)
    caches = [jnp.transpose(c, (0, 1, 3, 4, 2)).reshape(depth, n_pool, MIX_W, PAGE_SIZE)
              for c in (cache_diff_k, cache_diff_v, cache_sb_k, cache_sb_v)]

    xp = x_prompt.reshape(bsz * seq, d)
    xs = x_sample.reshape(nd, d)
    outs_p = [[] for _ in range(3)]
    kv_p = [jnp.zeros((depth, bsz * seq, MIX_W), F32) for _ in KV_COLS]
    outs_s = [[] for _ in range(7)]
    for l in range(depth):
        w_in_b = w_in[l].astype(BF16)
        wa = _block_diag(lru_wa[l]).astype(BF16)
        wx = _block_diag(lru_wx[l]).astype(BF16)
        lru = (conv_w[l], row(conv_b[l]), wa, row(lru_ba[l]), wx, row(lru_bx[l]), row(lru_lambda[l]))
        lam_vecs = jnp.stack([diff_lam_q1[l], diff_lam_k1[l], diff_lam_q2[l], diff_lam_k2[l]])
        consts = jnp.zeros((1, 128), F32).at[0, 0].set(0.8 - 0.6 * math.exp(-0.3 * l))
        dn_g = row(diff_norm_g[l])
        dn_g_t = jnp.tile(dn_g, (1, N_HEADS))
        hg_g_t = jnp.tile(row(hgrn_norm_g[l]), (1, N_HEADS))
        lb = row(hg_lb[l])
        wb = w_branch[l].astype(BF16)
        wo = w_out[l].astype(BF16)
        wr_hi, wr_lo, br = _router_matrix(router_group_w[l], router_group_b[l], router_expert_w[l], router_expert_b[l])
        w1, w3, w2 = exp_w1[l].astype(BF16), exp_w3[l].astype(BF16), exp_w2[l].astype(BF16)
        merge_w = (w_in_b, wb, wo, row(ln1_g[l]), row(ln1_b[l]))
        moe_w = (wr_hi, wr_lo, br, w1, w3, w2, row(ln2_g[l]), row(ln2_b[l]))

        u, *kv_p = _in_proj_kv(xp, w_in_b, l, kv_p)
        u3 = u.reshape(bsz, seq, -1)
        ya, conv_p, lru_p = _rglru_prompt(u3, *lru)
        yb = _diff_prompt(u3, lam_vecs, consts, dn_g)
        yc = _sb_prompt(u3)
        yd, st_t = _hgrn_prompt(u3, lb, hg_g_t)
        flat = lambda a: a.reshape(bsz * seq, MIX_W)
        x1 = _merge(flat(ya), flat(yb), flat(yc), flat(yd), xp, *merge_w, alpha)
        xp = _moe(x1, *moe_w, alpha)
        st = jnp.stack([st_t[:, h * HEAD_DIM:(h + 1) * HEAD_DIM, h * HEAD_DIM:(h + 1) * HEAD_DIM]
                        for h in range(N_HEADS)], axis=1).swapaxes(-1, -2)
        for lst, val in zip(outs_p, (conv_p, lru_p.reshape(bsz, MIX_W), st)):
            lst.append(val)

        us = _in_proj(xs, w_in_b)
        ya, conv_s, lru_s = _rglru_decode(us, state_conv[l].reshape(nd, -1), state_lru[l], *lru)
        yb, yc = _decode_attn(l, page_table, caches, us.reshape(nd, 1, -1), lam_vecs, consts, dn_g_t)
        colb = lambda c: jnp.broadcast_to(us[:, c * MIX_W:(c + 1) * MIX_W, None], (nd, MIX_W, HEAD_DIM))
        v_e = jnp.broadcast_to(us[:, COL_HI * MIX_W:(COL_HI + 1) * MIX_W].reshape(nd, N_HEADS, 1, HEAD_DIM),
                               (nd, N_HEADS, HEAD_DIM, HEAD_DIM)).reshape(nd, MIX_W, HEAD_DIM)
        lb_e = jnp.broadcast_to(hg_lb[l][None, :, None], (1, MIX_W, HEAD_DIM))
        hgrn_s, o_d = _hgrn_decode(state_hgrn[l].reshape(nd, MIX_W, HEAD_DIM), colb(COL_HQ), colb(COL_HF), v_e, lb_e)
        yd = _hgrn_decode_out(o_d.reshape(nd, MIX_W), us, hg_g_t)
        x1 = _merge(ya, yb.reshape(nd, MIX_W), yc.reshape(nd, MIX_W), yd, xs, *merge_w, alpha)
        xs = _moe(x1, *moe_w, alpha)
        heads = lambda c: us[:, c * MIX_W:(c + 1) * MIX_W].reshape(nd, 1, N_HEADS, HEAD_DIM)
        for lst, val in zip(outs_s, (heads(COL_DK), heads(COL_DV), heads(COL_SK), heads(COL_SV),
                                     conv_s.reshape(nd, CONV_WIDTH - 1, MIX_W), lru_s,
                                     hgrn_s.reshape(nd, N_HEADS, HEAD_DIM, HEAD_DIM))):
            lst.append(val)

    stack = lambda lsts: [jnp.stack(v, axis=0) for v in lsts]
    kv_p = [a.reshape(depth, bsz, seq, N_HEADS, HEAD_DIM) for a in kv_p]
    return (xp.reshape(bsz, seq, d), xs.reshape(nd, 1, d), *kv_p, *stack(outs_p), *stack(outs_s))
```

```python
import functools
import math

import jax
import jax.numpy as jnp
from jax import lax
from jax.experimental import pallas as pl
from jax.experimental.pallas import tpu as pltpu

F32 = jnp.float32
BF16 = jnp.bfloat16

N_BRANCH = 4
MIX_W = 256
N_HEADS = 4
HEAD_DIM = 64
DIFF_QK_DIM = 32
CONV_WIDTH = 4
LRU_C = 8.0
N_GROUPS = 4
EXPERTS_PER_GROUP = 4
N_EXPERTS = 16
LN_EPS = 1e-5
RMS_EPS = 1e-5
NEG_BIG = -1e30
F_FLOOR = 1e-30
PAGE_SIZE = 128
LOG2E = math.log2(math.e)
LANES = 128

COL_XA, COL_GA, COL_DQ, COL_DK, COL_DV, COL_SQ, COL_SK, COL_SV, COL_HQ, COL_HF, COL_HI, COL_HG = range(12)
N_MIX_COLS = 12

V7X_VMEM_BYTES = 64 * 1024 * 1024
VMEM_LIMIT = V7X_VMEM_BYTES - 12 * 1024 * 1024

HG_CHUNK = 64
HG_SUB = 16
ROUTER_LANES = 128
ROUTER_E0 = 16
ROW_CHUNK = 32


def _cparams(*sem):
    return pltpu.CompilerParams(dimension_semantics=sem, vmem_limit_bytes=VMEM_LIMIT)


def _dot(a, b):
    return jnp.dot(a, b, preferred_element_type=F32)


def _dot_nt(a, b):
    return lax.dot_general(a, b, (((1,), (1,)), ((), ())), preferred_element_type=F32)


def _dot_tn(a, b):
    return lax.dot_general(a, b, (((0,), (0,)), ((), ())), preferred_element_type=F32)


def _split_dot(x, w_bf16):
    hi = x.astype(BF16)
    lo = (x - hi.astype(F32)).astype(BF16)
    return _dot(hi, w_bf16) + _dot(lo, w_bf16)


def _sigmoid(x):
    return 1.0 / (1.0 + jnp.exp(-x))


def _silu(x):
    return x * _sigmoid(x)


def _gelu_tanh(x):
    c = math.sqrt(2.0 / math.pi)
    return 0.5 * x * (1.0 + jnp.tanh(c * (x + 0.044715 * (x * x * x))))


def _softplus(x):
    return jnp.maximum(x, 0.0) + jnp.log(1.0 + jnp.exp(-jnp.abs(x)))


def _iota(shape, dim):
    return lax.broadcasted_iota(jnp.int32, shape, dim)


def _head_ones(n):
    return jnp.where((_iota((n, n), 0) // HEAD_DIM) == (_iota((n, n), 1) // HEAD_DIM), 1.0, 0.0)


def _layer_norm(h, g, b):
    mu = jnp.mean(h, axis=-1, keepdims=True)
    d = h - mu
    var = jnp.mean(d * d, axis=-1, keepdims=True)
    return d * lax.rsqrt(var + LN_EPS) * g + b


def _lb_kernel(raw_ref, o_ref):
    raw = raw_ref[...]
    m = jnp.max(raw, axis=0, keepdims=True)
    e = jnp.exp(raw - m)
    soft = e / jnp.sum(e, axis=0, keepdims=True)
    rows, run = [], jnp.zeros_like(soft[0:1, :])
    for l in range(raw.shape[0]):
        run = run + soft[l:l + 1, :]
        rows.append(run)
    cum = jnp.concatenate(rows, axis=0)
    o_ref[...] = jnp.clip(cum - soft[0:1, :], 0.0, 1.0)


def _hgrn_lower_bounds(raw):
    return pl.pallas_call(_lb_kernel, out_shape=jax.ShapeDtypeStruct(raw.shape, F32))(raw)


def _mm_kernel(x_ref, w_ref, o_ref):
    o_ref[...] = _dot(x_ref[...].astype(BF16), w_ref[...])


def _in_proj(x, w, layer):
    n, k = x.shape
    c = N_MIX_COLS * MIX_W
    tm = min(1024, n)
    tn = 1024
    return pl.pallas_call(
        _mm_kernel,
        grid=(n // tm, c // tn),
        in_specs=[pl.BlockSpec((tm, k), lambda i, j: (i, 0)),
                  pl.BlockSpec((None, k, tn), lambda i, j: (layer, 0, j))],
        out_specs=pl.BlockSpec((tm, tn), lambda i, j: (i, j)),
        out_shape=jax.ShapeDtypeStruct((n, c), F32),
        compiler_params=_cparams("parallel", "parallel"),
    )(x, w)


KV_COLS = (COL_DK, COL_DV, COL_SK, COL_SV)


def _mm_kv_kernel(x_ref, w_ref, *refs, tn):
    o_ref = refs[len(KV_COLS)]
    kv_refs = refs[len(KV_COLS) + 1:]
    j = pl.program_id(1)
    res = _dot(x_ref[...].astype(BF16), w_ref[...])
    o_ref[...] = res
    for kv_ref, col in zip(kv_refs, KV_COLS):
        tile, local = divmod(col, tn // MIX_W)

        @pl.when(j == tile)
        def _(kv_ref=kv_ref, local=local):
            kv_ref[...] = res[:, local * MIX_W:(local + 1) * MIX_W]


def _in_proj_kv(x, w, layer, kv_stacks):
    n, k = x.shape
    c = N_MIX_COLS * MIX_W
    tm = min(1024, n)
    tn = 1024
    kv_spec = pl.BlockSpec((None, tm, MIX_W), lambda i, j: (layer, i, 0))
    n_kv = len(KV_COLS)
    return pl.pallas_call(
        functools.partial(_mm_kv_kernel, tn=tn),
        grid=(n // tm, c // tn),
        in_specs=[pl.BlockSpec((tm, k), lambda i, j: (i, 0)),
                  pl.BlockSpec((None, k, tn), lambda i, j: (layer, 0, j))] + [pl.BlockSpec(memory_space=pl.ANY)] * n_kv,
        out_specs=[pl.BlockSpec((tm, tn), lambda i, j: (i, j))] + [kv_spec] * n_kv,
        out_shape=[jax.ShapeDtypeStruct((n, c), F32)] + [jax.ShapeDtypeStruct(s.shape, F32) for s in kv_stacks],
        input_output_aliases={2 + a: 1 + a for a in range(n_kv)},
        compiler_params=_cparams("parallel", "arbitrary"),
    )(x, w, *kv_stacks)


def _lru_gates(xc, wa, ba, wx, bx, lam):
    xcb = xc.astype(BF16)
    r = _sigmoid(_dot(xcb, wa) + ba)
    i_g = _sigmoid(_dot(xcb, wx) + bx)
    log_a = -LRU_C * r * _softplus(-lam)
    a = jnp.exp(log_a)
    mult = jnp.sqrt(jnp.maximum(1.0 - jnp.exp(2.0 * log_a), 0.0))
    return a, mult, i_g


def _rglru_prompt_kernel(xa_ref, ga_ref, cw_ref, cb_ref, wa_ref, ba_ref, wx_ref, bx_ref, lam_ref,
                         y_ref, conv_ref, h_ref, xbuf, sa, sb, hc):
    t = pl.program_id(1)
    tt = xa_ref.shape[0]
    pad = tt // 2

    @pl.when(t == 0)
    def _():
        xbuf[0:8, :] = jnp.zeros((8, MIX_W), F32)
        hc[...] = jnp.zeros_like(hc)

    sa[0:pad, :] = jnp.ones((pad, MIX_W), F32)
    sb[0:pad, :] = jnp.zeros((pad, MIX_W), F32)

    xa = xa_ref[...]
    xbuf[8:8 + tt, :] = xa
    xc = cb_ref[...] + cw_ref[CONV_WIDTH - 1:CONV_WIDTH, :] * xa
    for i in range(CONV_WIDTH - 1):
        xc = xc + cw_ref[i:i + 1, :] * xbuf[5 + i:5 + i + tt, :]
    a, mult, i_g = _lru_gates(xc, wa_ref[...], ba_ref[...], wx_ref[...], bx_ref[...], lam_ref[...])
    pos = _iota((tt, 1), 0) + t * tt
    mult = jnp.where(pos == 0, 1.0, mult)
    b = mult * i_g * xc
    sa[pad:pad + tt, :] = a
    sb[pad:pad + tt, :] = b
    sb[pad:pad + 1, :] = b[0:1, :] + a[0:1, :] * hc[...]

    d = 1
    while d < tt:
        a_cur = sa[pad:pad + tt, :]
        b_cur = sb[pad:pad + tt, :]
        a_sh = sa[pad - d:pad - d + tt, :]
        b_sh = sb[pad - d:pad - d + tt, :]
        sb[pad:pad + tt, :] = a_cur * b_sh + b_cur
        if 2 * d < tt:
            sa[pad:pad + tt, :] = a_cur * a_sh
        d *= 2

    h = sb[pad:pad + tt, :]
    y_ref[...] = _gelu_tanh(ga_ref[...]) * h
    hc[...] = h[tt - 1:tt, :]
    xbuf[0:8, :] = xa[tt - 8:tt, :]
    conv_ref[...] = xa[tt - (CONV_WIDTH - 1):tt, :]
    h_ref[...] = h[tt - 1:tt, :]


def _rglru_prompt(u3, cw, cb, wa, ba, wx, bx, lam, tt=512):
    bsz, t, _ = u3.shape
    full = lambda shape: pl.BlockSpec(shape, lambda b, i: (0,) * len(shape))
    return pl.pallas_call(
        _rglru_prompt_kernel,
        grid=(bsz, t // tt),
        in_specs=[pl.BlockSpec((None, tt, MIX_W), lambda b, i: (b, i, COL_XA)),
                  pl.BlockSpec((None, tt, MIX_W), lambda b, i: (b, i, COL_GA)),
                  full((CONV_WIDTH, MIX_W)), full((1, MIX_W)), full((MIX_W, MIX_W)), full((1, MIX_W)),
                  full((MIX_W, MIX_W)), full((1, MIX_W)), full((1, MIX_W))],
        out_specs=[pl.BlockSpec((None, tt, MIX_W), lambda b, i: (b, i, 0)),
                   pl.BlockSpec((None, CONV_WIDTH - 1, MIX_W), lambda b, i: (b, 0, 0)),
                   pl.BlockSpec((None, 1, MIX_W), lambda b, i: (b, 0, 0))],
        out_shape=[jax.ShapeDtypeStruct((bsz, t, MIX_W), F32),
                   jax.ShapeDtypeStruct((bsz, CONV_WIDTH - 1, MIX_W), F32),
                   jax.ShapeDtypeStruct((bsz, 1, MIX_W), F32)],
        scratch_shapes=[pltpu.VMEM((tt + 8, MIX_W), F32),
                        pltpu.VMEM((tt + tt // 2, MIX_W), F32),
                        pltpu.VMEM((tt + tt // 2, MIX_W), F32),
                        pltpu.VMEM((1, MIX_W), F32)],
        compiler_params=_cparams("parallel", "arbitrary"),
    )(u3, u3, cw, cb, wa, ba, wx, bx, lam)


def _rglru_decode_kernel(xa_ref, ga_ref, conv_ref, h0_ref, cw_ref, cb_ref, wa_ref, ba_ref, wx_ref, bx_ref,
                         lam_ref, y_ref, convn_ref, h_ref):
    xa = xa_ref[...]
    w = MIX_W
    xc = cb_ref[...] + cw_ref[CONV_WIDTH - 1:CONV_WIDTH, :] * xa
    for i in range(CONV_WIDTH - 1):
        xc = xc + cw_ref[i:i + 1, :] * conv_ref[:, i * w:(i + 1) * w]
    a, mult, i_g = _lru_gates(xc, wa_ref[...], ba_ref[...], wx_ref[...], bx_ref[...], lam_ref[...])
    h = a * h0_ref[...] + mult * i_g * xc
    y_ref[...] = _gelu_tanh(ga_ref[...]) * h
    h_ref[...] = h
    convn_ref[:, 0:(CONV_WIDTH - 2) * w] = conv_ref[:, w:(CONV_WIDTH - 1) * w]
    convn_ref[:, (CONV_WIDTH - 2) * w:(CONV_WIDTH - 1) * w] = xa


def _rglru_decode(u, conv, h0, cw, cb, wa, ba, wx, bx, lam):
    n = u.shape[0]
    cwid = (CONV_WIDTH - 1) * MIX_W
    full = lambda shape: pl.BlockSpec(shape, lambda i: (0,) * len(shape))
    return pl.pallas_call(
        _rglru_decode_kernel,
        grid=(1,),
        in_specs=[pl.BlockSpec((n, MIX_W), lambda i: (0, COL_XA)),
                  pl.BlockSpec((n, MIX_W), lambda i: (0, COL_GA)),
                  full((n, cwid)), full((n, MIX_W)),
                  full((CONV_WIDTH, MIX_W)), full((1, MIX_W)), full((MIX_W, MIX_W)), full((1, MIX_W)),
                  full((MIX_W, MIX_W)), full((1, MIX_W)), full((1, MIX_W))],
        out_specs=[full((n, MIX_W)), full((n, cwid)), full((n, MIX_W))],
        out_shape=[jax.ShapeDtypeStruct((n, MIX_W), F32),
                   jax.ShapeDtypeStruct((n, cwid), F32),
                   jax.ShapeDtypeStruct((n, MIX_W), F32)],
        compiler_params=_cparams("arbitrary"),
    )(u, u, conv, h0, cw, cb, wa, ba, wx, bx, lam)


def _alibi_slope(h):
    return 2.0 ** (-8.0 * (h + 1) / N_HEADS)


def _diff_lambda(lam_ref, cst_ref):
    lv = lam_ref[...]
    s1 = jnp.sum(lv[0:1, :] * lv[1:2, :], axis=1, keepdims=True)
    s2 = jnp.sum(lv[2:3, :] * lv[3:4, :], axis=1, keepdims=True)
    lam_init = cst_ref[0:1, 0:1]
    return jnp.exp(s1) - jnp.exp(s2) + lam_init, lam_init


def _head_slab(h):
    return slice((h // 2) * LANES, (h // 2 + 1) * LANES), (h % 2) * HEAD_DIM


def _diff_q_rows(q, h):
    lane = _iota((1, LANES), 1)
    slab, lo = _head_slab(h)
    qs = q[:, slab] * (DIFF_QK_DIM ** -0.5 * LOG2E)
    q1 = jnp.where((lane >= lo) & (lane < lo + DIFF_QK_DIM), qs, 0.0)
    q2 = jnp.where((lane >= lo + DIFF_QK_DIM) & (lane < lo + HEAD_DIM), qs, 0.0)
    return jnp.concatenate([q1, q2], axis=0).astype(BF16)


def _diff_sum_lane(h):
    return HEAD_DIM if h % 2 == 0 else 0


def _diff_prompt_kernel(q_ref, k_ref, v_ref, lam_ref, cst_ref, g_ref, y_ref, kb_s, va_s, qs_s, m_s, acc_s):
    i = pl.program_id(1)
    tq = q_ref.shape[0]
    tk = tq
    t_all = k_ref.shape[0]
    rows = 2 * tq
    nl = tk // LANES

    @pl.when(i == 0)
    def _():
        lane = _iota((1, LANES), 1)

        def prep(c, carry):
            r0 = pl.multiple_of(c * tk, tk)
            kb_s[pl.ds(r0, tk), :] = k_ref[pl.ds(r0, tk), :].astype(BF16)
            v = v_ref[pl.ds(r0, tk), :]
            for h in range(N_HEADS):
                slab = v[:, (h // 2) * LANES:(h // 2 + 1) * LANES]
                va_s[h, pl.ds(r0, tk), :] = jnp.where(lane == _diff_sum_lane(h), 1.0, slab).astype(BF16)
            return carry

        lax.fori_loop(0, t_all // tk, prep, 0)

    q = q_ref[...]
    for h in range(N_HEADS):
        qs_s[h] = _diff_q_rows(q, h)
    m_s[...] = jnp.full(m_s.shape, NEG_BIG, F32)
    acc_s[...] = jnp.zeros_like(acc_s)
    row_in_tile = _iota((rows, LANES), 0) % tq
    col = _iota((rows, LANES), 1)

    def block(j, masked):
        k0 = pl.multiple_of(j * tk, tk)
        kpos = (_iota((1, tk), 1) + ((j - i) * tk - (tq - 1))).astype(F32)
        for h in range(N_HEADS):
            slab, _ = _head_slab(h)
            s = _dot_nt(qs_s[h], kb_s[pl.ds(k0, tk), slab])
            bias = kpos * (_alibi_slope(h) * LOG2E)
            cols = []
            for c in range(nl):
                sc = s[:, c * LANES:(c + 1) * LANES] + bias[:, c * LANES:(c + 1) * LANES]
                if masked:
                    sc = jnp.where(col + c * LANES <= row_in_tile, sc, NEG_BIG)
                cols.append(sc)
            mx = cols[0]
            for sc in cols[1:]:
                mx = jnp.maximum(mx, sc)
            m_prev = m_s[h]
            m_new = jnp.maximum(m_prev, jnp.max(mx, axis=1, keepdims=True))
            p = jnp.concatenate([jnp.exp2(sc - m_new) for sc in cols], axis=1).astype(BF16)
            acc_s[h] = jnp.exp2(m_prev - m_new) * acc_s[h] + _dot(p, va_s[h, pl.ds(k0, tk), :])
            m_s[h] = m_new

    def body(j, carry):
        block(j, False)
        return carry

    lax.fori_loop(0, i, body, 0)
    block(i, True)
    lam, lam_init = _diff_lambda(lam_ref, cst_ref)
    outs = []
    for h in range(N_HEADS):
        acc = acc_s[h]
        lo = (h % 2) * HEAD_DIM
        sl = _diff_sum_lane(h)
        o = acc[:, lo:lo + HEAD_DIM] / acc[:, sl:sl + 1]
        o = o[0:tq, :] - lam * o[tq:rows, :]
        inv = lax.rsqrt(jnp.mean(o * o, axis=1, keepdims=True) + RMS_EPS)
        outs.append(o * inv * g_ref[...] * (1.0 - lam_init))
    y_ref[...] = jnp.concatenate(outs, axis=1)


def _diff_prompt(u3, lam_vecs, consts, norm_g, tq=512):
    bsz, t, _ = u3.shape
    full = lambda shape: pl.BlockSpec(shape, lambda b, i: (0,) * len(shape))
    return pl.pallas_call(
        _diff_prompt_kernel,
        grid=(bsz, t // tq),
        in_specs=[pl.BlockSpec((None, tq, MIX_W), lambda b, i: (b, i, COL_DQ)),
                  pl.BlockSpec((None, t, MIX_W), lambda b, i: (b, 0, COL_DK)),
                  pl.BlockSpec((None, t, MIX_W), lambda b, i: (b, 0, COL_DV)),
                  full(lam_vecs.shape), full(consts.shape), full(norm_g.shape)],
        out_specs=pl.BlockSpec((None, tq, MIX_W), lambda b, i: (b, i, 0)),
        out_shape=jax.ShapeDtypeStruct((bsz, t, MIX_W), F32),
        scratch_shapes=[pltpu.VMEM((t, MIX_W), BF16), pltpu.VMEM((N_HEADS, t, LANES), BF16),
                        pltpu.VMEM((N_HEADS, 2 * tq, LANES), BF16),
                        pltpu.VMEM((N_HEADS, 2 * tq, LANES), F32), pltpu.VMEM((N_HEADS, 2 * tq, LANES), F32)],
        compiler_params=_cparams("parallel", "arbitrary"),
    )(u3, u3, u3, lam_vecs, consts, norm_g)


def _neg_log2_keep(z2):
    e = jnp.exp2(jnp.minimum(z2, -z2))
    return jnp.maximum(z2, 0.0) + jnp.log2(1.0 + e)


def _strict_upper(n):
    return jnp.where(_iota((n, n), 0) > _iota((n, n), 1), 1.0, 0.0).astype(BF16)


def _sb_prompt_kernel(q_ref, k_ref, v_ref, y_ref, kb_s, vb_s, qs_s, r_s, acc_s, hl_s, tot_s, w_s, *, tk):
    i = pl.program_id(1)
    tq = q_ref.shape[0]
    t_all = k_ref.shape[0]
    nl = tk // LANES
    per_tile = tq // tk

    @pl.when(i == 0)
    def _():
        def prep(c, carry):
            r0 = pl.multiple_of(c * tq, tq)
            kb_s[pl.ds(r0, tq), :] = k_ref[pl.ds(r0, tq), :].astype(BF16)
            vb_s[pl.ds(r0, tq), :] = v_ref[pl.ds(r0, tq), :].astype(BF16)
            return carry

        lax.fori_loop(0, t_all // tq, prep, 0)

    q = q_ref[...]
    lane = _iota((1, LANES), 1)
    for h in range(N_HEADS):
        slab, lo = _head_slab(h)
        qs_s[h] = jnp.where((lane >= lo) & (lane < lo + HEAD_DIM), q[:, slab] * (HEAD_DIM ** -0.5 * LOG2E),
                            0.0).astype(BF16)
    r_s[...] = jnp.zeros_like(r_s)
    acc_s[...] = jnp.zeros_like(acc_s)
    col = _iota((ROW_CHUNK, LANES), 1)
    row0 = _iota((ROW_CHUNK, LANES), 0)
    tri = jnp.where(_iota((2 * tk, tk), 0) % tk >= _iota((2 * tk, tk), 1), 1.0, 0.0).astype(BF16)
    chunks = [slice(r, r + ROW_CHUNK) for r in range(0, tq, ROW_CHUNK)]

    def block(j, key_off):
        k0 = pl.multiple_of(j * tk, tk)
        r0 = 0 if key_off is None else key_off
        rows = slice(r0, tq)
        blk_chunks = [rs for rs in chunks if rs.start >= r0]
        local = lambda rs: slice(rs.start - r0, rs.stop - r0)

        def earlier(rs, c):
            return col + (c * LANES + key_off) < row0 + rs.start

        z2s = [_dot_nt(qs_s[h, rows, :], kb_s[pl.ds(k0, tk), _head_slab(h)[0]])
               for h in range(N_HEADS)]
        for h in range(N_HEADS):
            for rs in blk_chunks:
                nk = _neg_log2_keep(z2s[h][local(rs), :])
                if key_off is not None:
                    nk = jnp.concatenate([jnp.where(earlier(rs, c), nk[:, c * LANES:(c + 1) * LANES], 0.0)
                                          for c in range(nl)], axis=1)
                hi = nk.astype(BF16)
                hl_s[h, rs, 0:tk] = hi
                hl_s[h, rs, tk:2 * tk] = (nk - hi.astype(F32)).astype(BF16)
                tot_s[h, rs, :] = jnp.broadcast_to(jnp.sum(nk, axis=1, keepdims=True), (ROW_CHUNK, LANES))
        incl = [_dot(hl_s[h, rows, :], tri) for h in range(N_HEADS)]
        for h in range(N_HEADS):
            for rs in blk_chunks:
                rr = r_s[h, rs, :]
                z2 = z2s[h][local(rs), :]
                inc = incl[h][local(rs), :]
                ws = []
                for c in range(nl):
                    sl = slice(c * LANES, (c + 1) * LANES)
                    wc = jnp.exp2(jnp.minimum(z2[:, sl] - inc[:, sl] - rr, 0.0))
                    if key_off is not None:
                        wc = jnp.where(earlier(rs, c), wc, 0.0)
                    ws.append(wc)
                w_s[h, rs, :] = jnp.concatenate(ws, axis=1).astype(BF16)
                r_s[h, rs, :] = rr + tot_s[h, rs, :]
            acc_s[h, rows, :] = acc_s[h, rows, :] + _dot(w_s[h, rows, :], vb_s[pl.ds(k0, tk), _head_slab(h)[0]])

    for d in range(per_tile - 1, -1, -1):
        block(i * per_tile + d, d * tk)

    def body(jj, carry):
        block(i * per_tile - 1 - jj, None)
        return carry

    lax.fori_loop(0, i * per_tile, body, 0)
    y_ref[...] = jnp.concatenate([acc_s[h][:, _head_slab(h)[1]:_head_slab(h)[1] + HEAD_DIM]
                                  for h in range(N_HEADS)], axis=1)


def _sb_prompt(u3, tq=512, tk=256):
    bsz, t, _ = u3.shape
    return pl.pallas_call(
        functools.partial(_sb_prompt_kernel, tk=tk),
        grid=(bsz, t // tq),
        in_specs=[pl.BlockSpec((None, tq, MIX_W), lambda b, i: (b, i, COL_SQ)),
                  pl.BlockSpec((None, t, MIX_W), lambda b, i: (b, 0, COL_SK)),
                  pl.BlockSpec((None, t, MIX_W), lambda b, i: (b, 0, COL_SV))],
        out_specs=pl.BlockSpec((None, tq, MIX_W), lambda b, i: (b, i, 0)),
        out_shape=jax.ShapeDtypeStruct((bsz, t, MIX_W), F32),
        scratch_shapes=[pltpu.VMEM((t, MIX_W), BF16), pltpu.VMEM((t, MIX_W), BF16),
                        pltpu.VMEM((N_HEADS, tq, LANES), BF16),
                        pltpu.VMEM((N_HEADS, tq, LANES), F32), pltpu.VMEM((N_HEADS, tq, LANES), F32),
                        pltpu.VMEM((N_HEADS, tq, 2 * tk), BF16), pltpu.VMEM((N_HEADS, tq, LANES), F32),
                        pltpu.VMEM((N_HEADS, tq, tk), BF16)],
        compiler_params=_cparams("parallel", "arbitrary"),
    )(u3, u3, u3)


def _decode_attn_one(dk_refs, dv_refs, sk_refs, sv_refs, dq, dkn, dvn, sq, lam_ref, cst_ref, g_ref):
    past = len(dk_refs) * PAGE_SIZE
    w = MIX_W
    lane = _iota((1, w), 1)
    row8 = _iota((2 * N_HEADS, 1), 0)
    cat = lambda page_refs: jnp.concatenate([r[...].astype(BF16) for r in page_refs], axis=1)

    q = dq * (DIFF_QK_DIM ** -0.5 * LOG2E)
    seg = lane // DIFF_QK_DIM
    qrows = jnp.where(seg == row8, q, 0.0)
    slope2 = jnp.exp((row8 // 2 + 1).astype(F32) * (-8.0 * math.log(2.0) / N_HEADS)) * LOG2E
    kpos = _iota((1, past), 1).astype(F32)
    sc = _dot(qrows.astype(BF16), cat(dk_refs)) - slope2 * (float(past) - kpos)
    s_new = jnp.sum(qrows * dkn, axis=1, keepdims=True)
    m = jnp.maximum(s_new, jnp.max(sc, axis=1, keepdims=True))
    p_new = jnp.exp2(s_new - m)
    pr = jnp.exp2(sc - m)
    l = p_new + jnp.sum(pr, axis=1, keepdims=True)
    acc = p_new * dvn + _dot_nt(pr.astype(BF16), cat(dv_refs))
    o = acc / l
    lam, lam_init = _diff_lambda(lam_ref, cst_ref)
    coef = jnp.where(row8 % 2 == 0, 1.0, -lam)
    head_of_lane = lane // HEAD_DIM
    o = jnp.where(head_of_lane == row8 // 2, o * coef, 0.0)
    o = jnp.sum(o, axis=0, keepdims=True)
    ms = _split_dot(o * o, _head_ones(w).astype(BF16)) * (1.0 / HEAD_DIM)
    yb = o * lax.rsqrt(ms + RMS_EPS) * g_ref[...] * (1.0 - lam_init)

    rowh = _iota((N_HEADS, 1), 0)
    qsb = jnp.where(head_of_lane == rowh, sq * (HEAD_DIM ** -0.5 * LOG2E), 0.0).astype(BF16)
    z2 = _dot(qsb, cat(sk_refs))
    nk = _neg_log2_keep(z2)
    chunk = 2 * PAGE_SIZE
    upper = _strict_upper(chunk)
    run = jnp.zeros((N_HEADS, 1), F32)
    later = [None] * (past // chunk)
    for c in range(past // chunk - 1, -1, -1):
        nk_c = nk[:, c * chunk:(c + 1) * chunk]
        later[c] = _split_dot(nk_c, upper) + run
        run = run + jnp.sum(nk_c, axis=1, keepdims=True)
    wgt = jnp.exp2(jnp.minimum(z2 - nk - jnp.concatenate(later, axis=1), 0.0))
    acc = _dot_nt(wgt.astype(BF16), cat(sv_refs))
    yc = jnp.sum(jnp.where(head_of_lane == rowh, acc, 0.0), axis=0, keepdims=True)
    return yb, yc


N_CACHES = 4


def _decode_attn_kernel(pt_ref, dkc_ref, dvc_ref, skc_ref, svc_ref, dq_ref, dkn_ref, dvn_ref, sq_ref,
                        lam_ref, cst_ref, g_ref, yb_ref, yc_ref, page_buf, page_sem, *, layer, n_pages, n_seq):
    b = pl.program_id(0)
    slot = lax.rem(b, 2)
    caches = (dkc_ref, dvc_ref, skc_ref, svc_ref)

    def page_copy(step, slot_, c, si, p):
        page = pt_ref[step * n_seq + si, p]
        return pltpu.make_async_copy(caches[c].at[layer, page], page_buf.at[slot_, c, si * n_pages + p],
                                     page_sem.at[slot_])

    def all_pages(step, slot_):
        return [page_copy(step, slot_, c, si, p)
                for c in range(N_CACHES) for si in range(n_seq) for p in range(n_pages)]

    @pl.when(b == 0)
    def _():
        for cp in all_pages(0, 0):
            cp.start()

    @pl.when(b + 1 < pl.num_programs(0))
    def _():
        for cp in all_pages(b + 1, 1 - slot):
            cp.start()

    for cp in all_pages(b, slot):
        cp.wait()

    for si in range(n_seq):
        pages = [[page_buf.at[slot, c, si * n_pages + p] for p in range(n_pages)] for c in range(N_CACHES)]
        yb, yc = _decode_attn_one(*pages, dq_ref[si], dkn_ref[si], dvn_ref[si], sq_ref[si],
                                  lam_ref, cst_ref, g_ref)
        yb_ref[si] = yb
        yc_ref[si] = yc


def _decode_attn(layer, page_table, caches, u3, lam_vecs, consts, norm_g_tiled, n_seq=2):
    n, n_pages = page_table.shape

    def col_spec(c):
        return pl.BlockSpec((n_seq, 1, MIX_W), lambda b, pt: (b, 0, c))

    full = lambda shape: pl.BlockSpec(shape, lambda b, pt: (0,) * len(shape))
    in_specs = [pl.BlockSpec(memory_space=pl.ANY)] * N_CACHES
    in_specs += [col_spec(COL_DQ), col_spec(COL_DK), col_spec(COL_DV), col_spec(COL_SQ),
                 full(lam_vecs.shape), full(consts.shape), full(norm_g_tiled.shape)]
    out_spec = pl.BlockSpec((n_seq, 1, MIX_W), lambda b, pt: (b, 0, 0))
    return pl.pallas_call(
        functools.partial(_decode_attn_kernel, layer=layer, n_pages=n_pages, n_seq=n_seq),
        grid_spec=pltpu.PrefetchScalarGridSpec(
            num_scalar_prefetch=1, grid=(n // n_seq,), in_specs=in_specs, out_specs=[out_spec, out_spec],
            scratch_shapes=[pltpu.VMEM((2, N_CACHES, n_seq * n_pages, MIX_W, PAGE_SIZE), F32),
                            pltpu.SemaphoreType.DMA((2,))]),
        out_shape=[jax.ShapeDtypeStruct((n, 1, MIX_W), F32), jax.ShapeDtypeStruct((n, 1, MIX_W), F32)],
        compiler_params=_cparams("arbitrary"),
    )(page_table, *caches, u3, u3, u3, u3, lam_vecs, consts, norm_g_tiled)


def _hgrn_gates(hq, hf, lb):
    q = _silu(hq)
    e = jnp.exp(-jnp.abs(hf))
    inv = 1.0 / (1.0 + e)
    pos = hf >= 0.0
    sig = jnp.where(pos, inv, e * inv)
    sig_n = jnp.where(pos, e * inv, inv)
    f = lb + (1.0 - lb) * sig
    return q, f, (1.0 - lb) * sig_n


def _hgrn_out(o, hg, g, ones_bf16):
    ms = _split_dot(o * o, ones_bf16) * (1.0 / HEAD_DIM)
    return o * lax.rsqrt(ms + RMS_EPS) * g * _silu(hg)


def _hgrn_prompt_kernel(hq_ref, hf_ref, hi_ref, hg_ref, lb_ref, g_ref, y_ref, st_ref, st_s):
    t = pl.program_id(1)
    tt = hq_ref.shape[0]
    c = HG_CHUNK
    w = MIX_W

    @pl.when(t == 0)
    def _():
        st_s[...] = jnp.zeros_like(st_s)

    lb = lb_ref[...]
    ones_f = _head_ones(w)
    ones_b = ones_f.astype(BF16)
    tril = jnp.where(_iota((c, c), 1) <= _iota((c, c), 0), 1.0, 0.0).astype(BF16)
    row = _iota((c, 1), 0)
    for ci in range(tt // c):
        sl = slice(ci * c, (ci + 1) * c)
        q, f, k = _hgrn_gates(hq_ref[sl, :], hf_ref[sl, :], lb)
        v = hi_ref[sl, :]
        cum = _split_dot_left(tril, jnp.log(jnp.maximum(f, F_FLOOR)))
        st = st_s[...]
        o = _dot_nt((q * jnp.exp(cum)).astype(BF16), st.astype(BF16))
        for j in range(c // HG_SUB - 1):
            s0, s1 = j * HG_SUB, (j + 1) * HG_SUB
            ref_row = cum[s1 - 1:s1, :]
            qj = q * jnp.exp(jnp.minimum(cum - ref_row, 0.0))
            kj = k[s0:s1, :] * jnp.exp(ref_row - cum[s0:s1, :])
            mt = _dot_tn(v[s0:s1, :].astype(BF16), kj.astype(BF16)) * ones_f
            oj = _dot_nt(qj.astype(BF16), mt.astype(BF16))
            o = o + jnp.where(row >= s1, oj, 0.0)
        for lag in range(HG_SUB):
            if lag == 0:
                ks, cs, vs = k, cum, v
            else:
                ks = pltpu.roll(k, lag, 0)
                cs = pltpu.roll(cum, lag, 0)
                vs = pltpu.roll(v, lag, 0)
            term = q * ks * jnp.exp(jnp.minimum(cum - cs, 0.0))
            ssum = _dot(term.astype(BF16), ones_b)
            o = o + jnp.where(row % HG_SUB >= lag, ssum * vs, 0.0)
        last = cum[c - 1:c, :]
        kc = k * jnp.exp(last - cum)
        st_s[...] = st * jnp.exp(last) + _dot_tn(v.astype(BF16), kc.astype(BF16)) * ones_f
        y_ref[sl, :] = _hgrn_out(o, hg_ref[sl, :], g_ref[...], ones_b)
    st_ref[...] = st_s[...]


def _split_dot_left(w_bf16, x):
    hi = x.astype(BF16)
    lo = (x - hi.astype(F32)).astype(BF16)
    return _dot(w_bf16, hi) + _dot(w_bf16, lo)


def _hgrn_prompt(u3, lb, g_tiled, tt=256):
    bsz, t, _ = u3.shape
    full = lambda shape: pl.BlockSpec(shape, lambda b, i: (0,) * len(shape))
    col = lambda c: pl.BlockSpec((None, tt, MIX_W), lambda b, i: (b, i, c))
    return pl.pallas_call(
        _hgrn_prompt_kernel,
        grid=(bsz, t // tt),
        in_specs=[col(COL_HQ), col(COL_HF), col(COL_HI), col(COL_HG), full((1, MIX_W)), full((1, MIX_W))],
        out_specs=[pl.BlockSpec((None, tt, MIX_W), lambda b, i: (b, i, 0)),
                   pl.BlockSpec((None, MIX_W, MIX_W), lambda b, i: (b, 0, 0))],
        out_shape=[jax.ShapeDtypeStruct((bsz, t, MIX_W), F32),
                   jax.ShapeDtypeStruct((bsz, MIX_W, MIX_W), F32)],
        scratch_shapes=[pltpu.VMEM((MIX_W, MIX_W), F32)],
        compiler_params=_cparams("parallel", "arbitrary"),
    )(u3, u3, u3, u3, lb, g_tiled)


def _hgrn_decode_kernel(s0_ref, q_ref, f_ref, v_ref, lb_ref, sn_ref, o_ref):
    bb = s0_ref.shape[0]
    q, f, k = _hgrn_gates(q_ref[...], f_ref[...], lb_ref[...])
    v = jnp.broadcast_to(v_ref[...], (bb, N_HEADS, HEAD_DIM, HEAD_DIM)).reshape(bb, MIX_W, HEAD_DIM)
    sn = f * s0_ref[...] + k * v
    sn_ref[...] = sn
    o_ref[...] = jnp.sum((q * sn).reshape(bb, N_HEADS, HEAD_DIM, HEAD_DIM), axis=2)


def _hgrn_decode(s0, q3, f3, v4, lb3, bb=8):
    n = s0.shape[0]
    blk = pl.BlockSpec((bb, MIX_W, HEAD_DIM), lambda i: (i, 0, 0))
    col = pl.BlockSpec((bb, MIX_W, 1), lambda i: (i, 0, 0))
    return pl.pallas_call(
        _hgrn_decode_kernel,
        grid=(n // bb,),
        in_specs=[blk, col, col, pl.BlockSpec((bb, N_HEADS, 1, HEAD_DIM), lambda i: (i, 0, 0, 0)),
                  pl.BlockSpec((1, MIX_W, 1), lambda i: (0, 0, 0))],
        out_specs=[blk, pl.BlockSpec((bb, N_HEADS, HEAD_DIM), lambda i: (i, 0, 0))],
        out_shape=[jax.ShapeDtypeStruct((n, MIX_W, HEAD_DIM), F32),
                   jax.ShapeDtypeStruct((n, N_HEADS, HEAD_DIM), F32)],
        compiler_params=_cparams("parallel"),
    )(s0, q3, f3, v4, lb3)


def _hgrn_out_kernel(o_ref, hg_ref, g_ref, y_ref):
    y_ref[...] = _hgrn_out(o_ref[...], hg_ref[...], g_ref[...], _head_ones(MIX_W).astype(BF16))


def _hgrn_decode_out(o, u, g_tiled):
    n = o.shape[0]
    full = lambda shape: pl.BlockSpec(shape, lambda i: (0,) * len(shape))
    return pl.pallas_call(
        _hgrn_out_kernel,
        grid=(1,),
        in_specs=[full((n, MIX_W)), pl.BlockSpec((n, MIX_W), lambda i: (0, COL_HG)), full((1, MIX_W))],
        out_specs=full((n, MIX_W)),
        out_shape=jax.ShapeDtypeStruct((n, MIX_W), F32),
        compiler_params=_cparams("arbitrary"),
    )(o, u, g_tiled)


def _merge_kernel(ya_ref, yb_ref, yc_ref, yd_ref, x_ref, wg0_ref, wg1_ref, wg2_ref, wg3_ref, wb_ref, wo_ref,
                  lg_ref, lbias_ref, o_ref, *, alpha):
    x = x_ref[...]
    xb = x.astype(BF16)
    merged = None
    for n, (y_ref, wg_ref) in enumerate(zip((ya_ref, yb_ref, yc_ref, yd_ref), (wg0_ref, wg1_ref, wg2_ref, wg3_ref))):
        gate = _sigmoid(_dot(xb, wg_ref[...]))
        term = gate * _dot(y_ref[...].astype(BF16), wb_ref[n])
        merged = term if merged is None else merged + term
    out = _dot(merged.astype(BF16), wo_ref[...])
    o_ref[...] = _layer_norm(alpha * x + out, lg_ref[...], lbias_ref[...])


def _merge(ya, yb, yc, yd, x, w_in, wb, wo, ln_g, ln_b, alpha, layer):
    n, d = x.shape
    tm = min(512, n)
    gate0 = N_MIX_COLS * MIX_W // d
    row = lambda width: pl.BlockSpec((tm, width), lambda i: (i, 0))
    full = lambda shape: pl.BlockSpec(shape, lambda i: (0,) * len(shape))
    gate = lambda b: pl.BlockSpec((None, d, d), lambda i, b=b: (layer, 0, gate0 + b))
    of_layer = lambda a: pl.BlockSpec((None,) + a.shape[1:], lambda i: (layer,) + (0,) * (a.ndim - 1))
    return pl.pallas_call(
        functools.partial(_merge_kernel, alpha=alpha),
        grid=(n // tm,),
        in_specs=[row(MIX_W), row(MIX_W), row(MIX_W), row(MIX_W), row(d),
                  gate(0), gate(1), gate(2), gate(3),
                  of_layer(wb), of_layer(wo), full((1, d)), full((1, d))],
        out_specs=row(d),
        out_shape=jax.ShapeDtypeStruct((n, d), F32),
        compiler_params=_cparams("parallel"),
    )(ya, yb, yc, yd, x, w_in, w_in, w_in, w_in, wb, wo, ln_g, ln_b)


def _router_weights(x, wr_hi, wr_lo, br):
    xh = x.astype(BF16)
    xl = (x - xh.astype(F32)).astype(BF16)
    logits = _dot(xh, wr_hi) + _dot(xh, wr_lo) + _dot(xl, wr_hi) + br
    lane = _iota((1, ROUTER_LANES), 1)
    big = jnp.int32(ROUTER_LANES)
    is_g = lane < N_GROUPS
    gl = jnp.where(is_g, logits, NEG_BIG)
    gmax = jnp.max(gl, axis=1, keepdims=True)
    g_idx = jnp.min(jnp.where(is_g & (gl == gmax), lane, big), axis=1, keepdims=True)
    g_w = 1.0 / jnp.sum(jnp.where(is_g, jnp.exp(gl - gmax), 0.0), axis=1, keepdims=True)
    in_grp = (lane >= ROUTER_E0) & (lane < ROUTER_E0 + N_EXPERTS) & \
             ((lane - ROUTER_E0) // EXPERTS_PER_GROUP == g_idx)
    el = jnp.where(in_grp, logits, NEG_BIG)
    v1 = jnp.max(el, axis=1, keepdims=True)
    i1 = jnp.min(jnp.where(in_grp & (el == v1), lane, big), axis=1, keepdims=True)
    el2 = jnp.where(lane == i1, NEG_BIG, el)
    v2 = jnp.max(el2, axis=1, keepdims=True)
    i2 = jnp.min(jnp.where(in_grp & (lane != i1) & (el2 == v2), lane, big), axis=1, keepdims=True)
    e2 = jnp.exp(v2 - v1)
    w1 = g_w / (1.0 + e2)
    w2 = g_w * e2 / (1.0 + e2)
    return jnp.where(lane == i1, w1, 0.0) + jnp.where(lane == i2, w2, 0.0)


def _moe_kernel(x_ref, wrh_ref, wrl_ref, br_ref, w1_ref, w3_ref, w2_ref, lg_ref, lbias_ref, o_ref,
                comb_s, acc_s, xb_s, *, alpha):
    g = pl.program_id(1)
    ff = w1_ref.shape[2]
    tm = x_ref.shape[0]

    @pl.when(g == 0)
    def _():
        x = x_ref[...]
        comb_s[...] = _router_weights(x, wrh_ref[...], wrl_ref[...], br_ref[...])
        xb_s[...] = x.astype(BF16)
        acc_s[...] = jnp.zeros_like(acc_s)

    lane = _iota((1, ROUTER_LANES), 1)
    comb = comb_s[...]
    cexp = []
    for e in range(EXPERTS_PER_GROUP):
        c_e = jnp.sum(jnp.where(lane == ROUTER_E0 + g * EXPERTS_PER_GROUP + e, comb, 0.0), axis=1, keepdims=True)
        cexp.append(jnp.broadcast_to(c_e, (tm, ff)))
    xb = xb_s[...]
    experts = range(EXPERTS_PER_GROUP)
    h1 = jnp.concatenate([_dot(xb, w1_ref[e]) for e in experts], axis=1)
    h3 = jnp.concatenate([_dot(xb, w3_ref[e]) for e in experts], axis=1)
    hid = _silu(h1) * h3 * jnp.concatenate(cexp, axis=1)
    acc_s[...] += _dot(hid.astype(BF16), w2_ref[...].reshape(EXPERTS_PER_GROUP * ff, -1))

    @pl.when(g == pl.num_programs(1) - 1)
    def _():
        o_ref[...] = _layer_norm(alpha * x_ref[...] + acc_s[...], lg_ref[...], lbias_ref[...])


def _moe(x, wr_hi, wr_lo, br, w1, w3, w2, ln_g, ln_b, alpha, layer):
    n, d = x.shape
    _, ne, _, ff = w1.shape
    tm = min(1024, n)
    epg = EXPERTS_PER_GROUP
    full = lambda shape: pl.BlockSpec(shape, lambda i, g: (0,) * len(shape))
    return pl.pallas_call(
        functools.partial(_moe_kernel, alpha=alpha),
        grid=(n // tm, ne // epg),
        in_specs=[pl.BlockSpec((tm, d), lambda i, g: (i, 0)),
                  full(wr_hi.shape), full(wr_lo.shape), full(br.shape),
                  pl.BlockSpec((None, epg, d, ff), lambda i, g: (layer, g, 0, 0)),
                  pl.BlockSpec((None, epg, d, ff), lambda i, g: (layer, g, 0, 0)),
                  pl.BlockSpec((None, epg, ff, d), lambda i, g: (layer, g, 0, 0)),
                  full((1, d)), full((1, d))],
        out_specs=pl.BlockSpec((tm, d), lambda i, g: (i, 0)),
        out_shape=jax.ShapeDtypeStruct((n, d), F32),
        scratch_shapes=[pltpu.VMEM((tm, ROUTER_LANES), F32), pltpu.VMEM((tm, d), F32),
                        pltpu.VMEM((tm, d), BF16)],
        compiler_params=_cparams("parallel", "arbitrary"),
    )(x, wr_hi, wr_lo, br, w1, w3, w2, ln_g, ln_b)


def _block_diag(w):
    nb, n, _ = w.shape
    eye = jnp.eye(nb, dtype=w.dtype)
    return (eye[:, None, :, None] * w[:, :, None, :]).reshape(nb * n, nb * n)


def _router_matrix(wg, bg, we, be):
    d = wg.shape[0]
    wr = jnp.zeros((d, ROUTER_LANES), F32)
    wr = wr.at[:, 0:N_GROUPS].set(wg).at[:, ROUTER_E0:ROUTER_E0 + N_EXPERTS].set(we)
    br = jnp.zeros((1, ROUTER_LANES), F32)
    br = br.at[0, 0:N_GROUPS].set(bg).at[0, ROUTER_E0:ROUTER_E0 + N_EXPERTS].set(be)
    hi = wr.astype(BF16)
    lo = (wr - hi.astype(F32)).astype(BF16)
    return hi, lo, br


def kernel(x_prompt, x_sample, cache_diff_k, cache_diff_v, cache_sb_k, cache_sb_v, page_table, state_conv, state_lru, state_hgrn, w_in, conv_w, conv_b, lru_wa, lru_ba, lru_wx, lru_bx, lru_lambda, diff_lam_q1, diff_lam_k1, diff_lam_q2, diff_lam_k2, diff_norm_g, hgrn_lb_raw, hgrn_norm_g, w_branch, w_out, ln1_g, ln1_b, router_group_w, router_group_b, router_expert_w, router_expert_b, exp_w1, exp_w3, exp_w2, ln2_g, ln2_b):
    depth = w_in.shape[0]
    bsz, seq, d = x_prompt.shape
    nd = x_sample.shape[0]
    n_pool = cache_diff_k.shape[1]
    alpha = (2 * depth) ** 0.25
    row = lambda a: a.reshape(1, -1)

    hg_lb = _hgrn_lower_bounds(hgrn_lb_raw)
    caches = [jnp.transpose(c, (0, 1, 3, 4, 2)).reshape(depth, n_pool, MIX_W, PAGE_SIZE)
              for c in (cache_diff_k, cache_diff_v, cache_sb_k, cache_sb_v)]

    xp = x_prompt.reshape(bsz * seq, d)
    xs = x_sample.reshape(nd, d)
    outs_p = [[] for _ in range(3)]
    kv_p = [jnp.zeros((depth, bsz * seq, MIX_W), F32) for _ in KV_COLS]
    outs_s = [[] for _ in range(7)]
    w_in_b = w_in.astype(BF16)
    wb, wo = w_branch.astype(BF16), w_out.astype(BF16)
    w1, w3, w2 = exp_w1.astype(BF16), exp_w3.astype(BF16), exp_w2.astype(BF16)
    for l in range(depth):
        wa = _block_diag(lru_wa[l]).astype(BF16)
        wx = _block_diag(lru_wx[l]).astype(BF16)
        lru = (conv_w[l], row(conv_b[l]), wa, row(lru_ba[l]), wx, row(lru_bx[l]), row(lru_lambda[l]))
        lam_vecs = jnp.stack([diff_lam_q1[l], diff_lam_k1[l], diff_lam_q2[l], diff_lam_k2[l]])
        consts = jnp.zeros((1, 128), F32).at[0, 0].set(0.8 - 0.6 * math.exp(-0.3 * l))
        dn_g = row(diff_norm_g[l])
        dn_g_t = jnp.tile(dn_g, (1, N_HEADS))
        hg_g_t = jnp.tile(row(hgrn_norm_g[l]), (1, N_HEADS))
        lb = row(hg_lb[l])
        wr_hi, wr_lo, br = _router_matrix(router_group_w[l], router_group_b[l], router_expert_w[l], router_expert_b[l])
        merge_w = (w_in_b, wb, wo, row(ln1_g[l]), row(ln1_b[l]), alpha, l)
        moe_w = (wr_hi, wr_lo, br, w1, w3, w2, row(ln2_g[l]), row(ln2_b[l]), alpha, l)

        u, *kv_p = _in_proj_kv(xp, w_in_b, l, kv_p)
        u3 = u.reshape(bsz, seq, -1)
        ya, conv_p, lru_p = _rglru_prompt(u3, *lru)
        yb = _diff_prompt(u3, lam_vecs, consts, dn_g)
        yc = _sb_prompt(u3)
        yd, st_t = _hgrn_prompt(u3, lb, hg_g_t)
        flat = lambda a: a.reshape(bsz * seq, MIX_W)
        x1 = _merge(flat(ya), flat(yb), flat(yc), flat(yd), xp, *merge_w)
        xp = _moe(x1, *moe_w)
        st = jnp.stack([st_t[:, h * HEAD_DIM:(h + 1) * HEAD_DIM, h * HEAD_DIM:(h + 1) * HEAD_DIM]
                        for h in range(N_HEADS)], axis=1).swapaxes(-1, -2)
        for lst, val in zip(outs_p, (conv_p, lru_p.reshape(bsz, MIX_W), st)):
            lst.append(val)

        us = _in_proj(xs, w_in_b, l)
        ya, conv_s, lru_s = _rglru_decode(us, state_conv[l].reshape(nd, -1), state_lru[l], *lru)
        yb, yc = _decode_attn(l, page_table, caches, us.reshape(nd, 1, -1), lam_vecs, consts, dn_g_t)
        colb = lambda c: us[:, c * MIX_W:(c + 1) * MIX_W].reshape(nd, MIX_W, 1)
        v4 = us[:, COL_HI * MIX_W:(COL_HI + 1) * MIX_W].reshape(nd, N_HEADS, 1, HEAD_DIM)
        hgrn_s, o_d = _hgrn_decode(state_hgrn[l].reshape(nd, MIX_W, HEAD_DIM), colb(COL_HQ), colb(COL_HF), v4,
                                   hg_lb[l].reshape(1, MIX_W, 1))
        yd = _hgrn_decode_out(o_d.reshape(nd, MIX_W), us, hg_g_t)
        x1 = _merge(ya, yb.reshape(nd, MIX_W), yc.reshape(nd, MIX_W), yd, xs, *merge_w)
        xs = _moe(x1, *moe_w)
        heads = lambda c: us[:, c * MIX_W:(c + 1) * MIX_W].reshape(nd, 1, N_HEADS, HEAD_DIM)
        for lst, val in zip(outs_s, (heads(COL_DK), heads(COL_DV), heads(COL_SK), heads(COL_SV),
                                     conv_s.reshape(nd, CONV_WIDTH - 1, MIX_W), lru_s,
                                     hgrn_s.reshape(nd, N_HEADS, HEAD_DIM, HEAD_DIM))):
            lst.append(val)

    stack = lambda lsts: [jnp.stack(v, axis=0) for v in lsts]
    kv_p = [a.reshape(depth, bsz, seq, N_HEADS, HEAD_DIM) for a in kv_p]
    return (xp.reshape(bsz, seq, d), xs.reshape(nd, 1, d), *kv_p, *stack(outs_p), *stack(outs_s))
```

```python
import functools
import math

import jax
import jax.numpy as jnp
from jax import lax
from jax.experimental import pallas as pl
from jax.experimental.pallas import tpu as pltpu

F32 = jnp.float32
BF16 = jnp.bfloat16

N_BRANCH = 4
MIX_W = 256
N_HEADS = 4
HEAD_DIM = 64
DIFF_QK_DIM = 32
CONV_WIDTH = 4
LRU_C = 8.0
N_GROUPS = 4
EXPERTS_PER_GROUP = 4
N_EXPERTS = 16
LN_EPS = 1e-5
RMS_EPS = 1e-5
NEG_BIG = -1e30
F_FLOOR = 1e-30
PAGE_SIZE = 128
LOG2E = math.log2(math.e)
LANES = 128

COL_XA, COL_GA, COL_DQ, COL_DK, COL_DV, COL_SQ, COL_SK, COL_SV, COL_HQ, COL_HF, COL_HI, COL_HG = range(12)
N_MIX_COLS = 12

V7X_VMEM_BYTES = 64 * 1024 * 1024
VMEM_LIMIT = V7X_VMEM_BYTES - 12 * 1024 * 1024

HG_CHUNK = 64
HG_SUB = 16
ROUTER_LANES = 128
ROUTER_E0 = 16
ROW_CHUNK = 32


def _cparams(*sem):
    return pltpu.CompilerParams(dimension_semantics=sem, vmem_limit_bytes=VMEM_LIMIT)


def _dot(a, b):
    return jnp.dot(a, b, preferred_element_type=F32)


def _dot_nt(a, b):
    return lax.dot_general(a, b, (((1,), (1,)), ((), ())), preferred_element_type=F32)


def _dot_tn(a, b):
    return lax.dot_general(a, b, (((0,), (0,)), ((), ())), preferred_element_type=F32)


def _split_dot(x, w_bf16):
    hi = x.astype(BF16)
    lo = (x - hi.astype(F32)).astype(BF16)
    return _dot(hi, w_bf16) + _dot(lo, w_bf16)


def _sigmoid(x):
    return 1.0 / (1.0 + jnp.exp(-x))


def _silu(x):
    return x * _sigmoid(x)


def _gelu_tanh(x):
    c = math.sqrt(2.0 / math.pi)
    return 0.5 * x * (1.0 + jnp.tanh(c * (x + 0.044715 * (x * x * x))))


def _softplus(x):
    return jnp.maximum(x, 0.0) + jnp.log(1.0 + jnp.exp(-jnp.abs(x)))


def _iota(shape, dim):
    return lax.broadcasted_iota(jnp.int32, shape, dim)


def _head_ones(n):
    return jnp.where((_iota((n, n), 0) // HEAD_DIM) == (_iota((n, n), 1) // HEAD_DIM), 1.0, 0.0)


def _layer_norm(h, g, b):
    mu = jnp.mean(h, axis=-1, keepdims=True)
    d = h - mu
    var = jnp.mean(d * d, axis=-1, keepdims=True)
    return d * lax.rsqrt(var + LN_EPS) * g + b


def _lb_kernel(raw_ref, o_ref):
    raw = raw_ref[...]
    m = jnp.max(raw, axis=0, keepdims=True)
    e = jnp.exp(raw - m)
    soft = e / jnp.sum(e, axis=0, keepdims=True)
    rows, run = [], jnp.zeros_like(soft[0:1, :])
    for l in range(raw.shape[0]):
        run = run + soft[l:l + 1, :]
        rows.append(run)
    cum = jnp.concatenate(rows, axis=0)
    o_ref[...] = jnp.clip(cum - soft[0:1, :], 0.0, 1.0)


def _hgrn_lower_bounds(raw):
    return pl.pallas_call(_lb_kernel, out_shape=jax.ShapeDtypeStruct(raw.shape, F32))(raw)


def _mm_kernel(x_ref, w_ref, o_ref):
    o_ref[...] = _dot(x_ref[...].astype(BF16), w_ref[...])


def _in_proj(x, w, layer):
    n, k = x.shape
    c = N_MIX_COLS * MIX_W
    tm = min(1024, n)
    tn = 1024
    return pl.pallas_call(
        _mm_kernel,
        grid=(n // tm, c // tn),
        in_specs=[pl.BlockSpec((tm, k), lambda i, j: (i, 0)),
                  pl.BlockSpec((None, k, tn), lambda i, j: (layer, 0, j))],
        out_specs=pl.BlockSpec((tm, tn), lambda i, j: (i, j)),
        out_shape=jax.ShapeDtypeStruct((n, c), F32),
        compiler_params=_cparams("parallel", "parallel"),
    )(x, w)


KV_COLS = (COL_DK, COL_DV, COL_SK, COL_SV)


def _mm_kv_kernel(x_ref, w_ref, *refs, tn):
    o_ref = refs[len(KV_COLS)]
    kv_refs = refs[len(KV_COLS) + 1:]
    j = pl.program_id(1)
    res = _dot(x_ref[...].astype(BF16), w_ref[...])
    o_ref[...] = res
    for kv_ref, col in zip(kv_refs, KV_COLS):
        tile, local = divmod(col, tn // MIX_W)

        @pl.when(j == tile)
        def _(kv_ref=kv_ref, local=local):
            kv_ref[...] = res[:, local * MIX_W:(local + 1) * MIX_W].T


def _in_proj_kv(x, w, layer, kv_stacks):
    n, k = x.shape
    c = N_MIX_COLS * MIX_W
    seq = kv_stacks[0].shape[-1]
    tm = min(1024, seq)
    tn = 1024
    per_seq = seq // tm
    kv_spec = pl.BlockSpec((None, None, MIX_W, tm), lambda i, j: (layer, i // per_seq, 0, i % per_seq))
    n_kv = len(KV_COLS)
    return pl.pallas_call(
        functools.partial(_mm_kv_kernel, tn=tn),
        grid=(n // tm, c // tn),
        in_specs=[pl.BlockSpec((tm, k), lambda i, j: (i, 0)),
                  pl.BlockSpec((None, k, tn), lambda i, j: (layer, 0, j))] + [pl.BlockSpec(memory_space=pl.ANY)] * n_kv,
        out_specs=[pl.BlockSpec((tm, tn), lambda i, j: (i, j))] + [kv_spec] * n_kv,
        out_shape=[jax.ShapeDtypeStruct((n, c), F32)] + [jax.ShapeDtypeStruct(s.shape, F32) for s in kv_stacks],
        input_output_aliases={2 + a: 1 + a for a in range(n_kv)},
        compiler_params=_cparams("parallel", "arbitrary"),
    )(x, w, *kv_stacks)


def _lru_gates(xc, wa, ba, wx, bx, lam):
    xcb = xc.astype(BF16)
    r = _sigmoid(_dot(xcb, wa) + ba)
    i_g = _sigmoid(_dot(xcb, wx) + bx)
    log_a = -LRU_C * r * _softplus(-lam)
    a = jnp.exp(log_a)
    mult = jnp.sqrt(jnp.maximum(1.0 - jnp.exp(2.0 * log_a), 0.0))
    return a, mult, i_g


def _rglru_prompt_kernel(xa_ref, ga_ref, cw_ref, cb_ref, wa_ref, ba_ref, wx_ref, bx_ref, lam_ref,
                         y_ref, conv_ref, h_ref, xbuf, sa, sb, hc):
    t = pl.program_id(1)
    tt = xa_ref.shape[0]
    pad = tt // 2

    @pl.when(t == 0)
    def _():
        xbuf[0:8, :] = jnp.zeros((8, MIX_W), F32)
        hc[...] = jnp.zeros_like(hc)

    sa[0:pad, :] = jnp.ones((pad, MIX_W), F32)
    sb[0:pad, :] = jnp.zeros((pad, MIX_W), F32)

    xa = xa_ref[...]
    xbuf[8:8 + tt, :] = xa
    xc = cb_ref[...] + cw_ref[CONV_WIDTH - 1:CONV_WIDTH, :] * xa
    for i in range(CONV_WIDTH - 1):
        xc = xc + cw_ref[i:i + 1, :] * xbuf[5 + i:5 + i + tt, :]
    a, mult, i_g = _lru_gates(xc, wa_ref[...], ba_ref[...], wx_ref[...], bx_ref[...], lam_ref[...])
    pos = _iota((tt, 1), 0) + t * tt
    mult = jnp.where(pos == 0, 1.0, mult)
    b = mult * i_g * xc
    sa[pad:pad + tt, :] = a
    sb[pad:pad + tt, :] = b
    sb[pad:pad + 1, :] = b[0:1, :] + a[0:1, :] * hc[...]

    d = 1
    while d < tt:
        a_cur = sa[pad:pad + tt, :]
        b_cur = sb[pad:pad + tt, :]
        a_sh = sa[pad - d:pad - d + tt, :]
        b_sh = sb[pad - d:pad - d + tt, :]
        sb[pad:pad + tt, :] = a_cur * b_sh + b_cur
        if 2 * d < tt:
            sa[pad:pad + tt, :] = a_cur * a_sh
        d *= 2

    h = sb[pad:pad + tt, :]
    y_ref[...] = _gelu_tanh(ga_ref[...]) * h
    hc[...] = h[tt - 1:tt, :]
    xbuf[0:8, :] = xa[tt - 8:tt, :]
    conv_ref[...] = xa[tt - (CONV_WIDTH - 1):tt, :]
    h_ref[...] = h[tt - 1:tt, :]


def _rglru_prompt(u3, cw, cb, wa, ba, wx, bx, lam, tt=512):
    bsz, t, _ = u3.shape
    full = lambda shape: pl.BlockSpec(shape, lambda b, i: (0,) * len(shape))
    return pl.pallas_call(
        _rglru_prompt_kernel,
        grid=(bsz, t // tt),
        in_specs=[pl.BlockSpec((None, tt, MIX_W), lambda b, i: (b, i, COL_XA)),
                  pl.BlockSpec((None, tt, MIX_W), lambda b, i: (b, i, COL_GA)),
                  full((CONV_WIDTH, MIX_W)), full((1, MIX_W)), full((MIX_W, MIX_W)), full((1, MIX_W)),
                  full((MIX_W, MIX_W)), full((1, MIX_W)), full((1, MIX_W))],
        out_specs=[pl.BlockSpec((None, tt, MIX_W), lambda b, i: (b, i, 0)),
                   pl.BlockSpec((None, CONV_WIDTH - 1, MIX_W), lambda b, i: (b, 0, 0)),
                   pl.BlockSpec((None, 1, MIX_W), lambda b, i: (b, 0, 0))],
        out_shape=[jax.ShapeDtypeStruct((bsz, t, MIX_W), F32),
                   jax.ShapeDtypeStruct((bsz, CONV_WIDTH - 1, MIX_W), F32),
                   jax.ShapeDtypeStruct((bsz, 1, MIX_W), F32)],
        scratch_shapes=[pltpu.VMEM((tt + 8, MIX_W), F32),
                        pltpu.VMEM((tt + tt // 2, MIX_W), F32),
                        pltpu.VMEM((tt + tt // 2, MIX_W), F32),
                        pltpu.VMEM((1, MIX_W), F32)],
        compiler_params=_cparams("parallel", "arbitrary"),
    )(u3, u3, cw, cb, wa, ba, wx, bx, lam)


def _rglru_decode_kernel(xa_ref, ga_ref, conv_ref, h0_ref, cw_ref, cb_ref, wa_ref, ba_ref, wx_ref, bx_ref,
                         lam_ref, y_ref, convn_ref, h_ref):
    xa = xa_ref[...]
    w = MIX_W
    xc = cb_ref[...] + cw_ref[CONV_WIDTH - 1:CONV_WIDTH, :] * xa
    for i in range(CONV_WIDTH - 1):
        xc = xc + cw_ref[i:i + 1, :] * conv_ref[:, i * w:(i + 1) * w]
    a, mult, i_g = _lru_gates(xc, wa_ref[...], ba_ref[...], wx_ref[...], bx_ref[...], lam_ref[...])
    h = a * h0_ref[...] + mult * i_g * xc
    y_ref[...] = _gelu_tanh(ga_ref[...]) * h
    h_ref[...] = h
    convn_ref[:, 0:(CONV_WIDTH - 2) * w] = conv_ref[:, w:(CONV_WIDTH - 1) * w]
    convn_ref[:, (CONV_WIDTH - 2) * w:(CONV_WIDTH - 1) * w] = xa


def _rglru_decode(u, conv, h0, cw, cb, wa, ba, wx, bx, lam):
    n = u.shape[0]
    cwid = (CONV_WIDTH - 1) * MIX_W
    full = lambda shape: pl.BlockSpec(shape, lambda i: (0,) * len(shape))
    return pl.pallas_call(
        _rglru_decode_kernel,
        grid=(1,),
        in_specs=[pl.BlockSpec((n, MIX_W), lambda i: (0, COL_XA)),
                  pl.BlockSpec((n, MIX_W), lambda i: (0, COL_GA)),
                  full((n, cwid)), full((n, MIX_W)),
                  full((CONV_WIDTH, MIX_W)), full((1, MIX_W)), full((MIX_W, MIX_W)), full((1, MIX_W)),
                  full((MIX_W, MIX_W)), full((1, MIX_W)), full((1, MIX_W))],
        out_specs=[full((n, MIX_W)), full((n, cwid)), full((n, MIX_W))],
        out_shape=[jax.ShapeDtypeStruct((n, MIX_W), F32),
                   jax.ShapeDtypeStruct((n, cwid), F32),
                   jax.ShapeDtypeStruct((n, MIX_W), F32)],
        compiler_params=_cparams("arbitrary"),
    )(u, u, conv, h0, cw, cb, wa, ba, wx, bx, lam)


def _alibi_slope(h):
    return 2.0 ** (-8.0 * (h + 1) / N_HEADS)


def _diff_lambda(lam_ref, cst_ref):
    lv = lam_ref[...]
    s1 = jnp.sum(lv[0:1, :] * lv[1:2, :], axis=1, keepdims=True)
    s2 = jnp.sum(lv[2:3, :] * lv[3:4, :], axis=1, keepdims=True)
    lam_init = cst_ref[0:1, 0:1]
    return jnp.exp(s1) - jnp.exp(s2) + lam_init, lam_init


def _head_slab(h):
    return slice((h // 2) * LANES, (h // 2 + 1) * LANES), (h % 2) * HEAD_DIM


def _diff_q_rows(q, h):
    lane = _iota((1, LANES), 1)
    slab, lo = _head_slab(h)
    qs = q[:, slab] * (DIFF_QK_DIM ** -0.5 * LOG2E)
    q1 = jnp.where((lane >= lo) & (lane < lo + DIFF_QK_DIM), qs, 0.0)
    q2 = jnp.where((lane >= lo + DIFF_QK_DIM) & (lane < lo + HEAD_DIM), qs, 0.0)
    return jnp.concatenate([q1, q2], axis=0).astype(BF16)


def _diff_sum_lane(h):
    return HEAD_DIM if h % 2 == 0 else 0


def _diff_prompt_kernel(q_ref, k_ref, v_ref, lam_ref, cst_ref, g_ref, y_ref, kb_s, va_s, qs_s, m_s, acc_s):
    i = pl.program_id(1)
    tq = q_ref.shape[0]
    tk = tq
    t_all = k_ref.shape[0]
    rows = 2 * tq
    nl = tk // LANES

    @pl.when(i == 0)
    def _():
        lane = _iota((1, LANES), 1)

        def prep(c, carry):
            r0 = pl.multiple_of(c * tk, tk)
            kb_s[pl.ds(r0, tk), :] = k_ref[pl.ds(r0, tk), :].astype(BF16)
            v = v_ref[pl.ds(r0, tk), :]
            for h in range(N_HEADS):
                slab = v[:, (h // 2) * LANES:(h // 2 + 1) * LANES]
                va_s[h, pl.ds(r0, tk), :] = jnp.where(lane == _diff_sum_lane(h), 1.0, slab).astype(BF16)
            return carry

        lax.fori_loop(0, t_all // tk, prep, 0)

    q = q_ref[...]
    for h in range(N_HEADS):
        qs_s[h] = _diff_q_rows(q, h)
    m_s[...] = jnp.full(m_s.shape, NEG_BIG, F32)
    acc_s[...] = jnp.zeros_like(acc_s)
    row_in_tile = _iota((rows, LANES), 0) % tq
    col = _iota((rows, LANES), 1)

    def block(j, masked):
        k0 = pl.multiple_of(j * tk, tk)
        kpos = (_iota((1, tk), 1) + ((j - i) * tk - (tq - 1))).astype(F32)
        for h in range(N_HEADS):
            slab, _ = _head_slab(h)
            s = _dot_nt(qs_s[h], kb_s[pl.ds(k0, tk), slab])
            bias = kpos * (_alibi_slope(h) * LOG2E)
            cols = []
            for c in range(nl):
                sc = s[:, c * LANES:(c + 1) * LANES] + bias[:, c * LANES:(c + 1) * LANES]
                if masked:
                    sc = jnp.where(col + c * LANES <= row_in_tile, sc, NEG_BIG)
                cols.append(sc)
            mx = cols[0]
            for sc in cols[1:]:
                mx = jnp.maximum(mx, sc)
            m_prev = m_s[h]
            m_new = jnp.maximum(m_prev, jnp.max(mx, axis=1, keepdims=True))
            p = jnp.concatenate([jnp.exp2(sc - m_new) for sc in cols], axis=1).astype(BF16)
            acc_s[h] = jnp.exp2(m_prev - m_new) * acc_s[h] + _dot(p, va_s[h, pl.ds(k0, tk), :])
            m_s[h] = m_new

    def body(j, carry):
        block(j, False)
        return carry

    lax.fori_loop(0, i, body, 0)
    block(i, True)
    lam, lam_init = _diff_lambda(lam_ref, cst_ref)
    outs = []
    for h in range(N_HEADS):
        acc = acc_s[h]
        lo = (h % 2) * HEAD_DIM
        sl = _diff_sum_lane(h)
        o = acc[:, lo:lo + HEAD_DIM] / acc[:, sl:sl + 1]
        o = o[0:tq, :] - lam * o[tq:rows, :]
        inv = lax.rsqrt(jnp.mean(o * o, axis=1, keepdims=True) + RMS_EPS)
        outs.append(o * inv * g_ref[...] * (1.0 - lam_init))
    y_ref[...] = jnp.concatenate(outs, axis=1)


def _diff_prompt(u3, lam_vecs, consts, norm_g, tq=512):
    bsz, t, _ = u3.shape
    full = lambda shape: pl.BlockSpec(shape, lambda b, i: (0,) * len(shape))
    return pl.pallas_call(
        _diff_prompt_kernel,
        grid=(bsz, t // tq),
        in_specs=[pl.BlockSpec((None, tq, MIX_W), lambda b, i: (b, i, COL_DQ)),
                  pl.BlockSpec((None, t, MIX_W), lambda b, i: (b, 0, COL_DK)),
                  pl.BlockSpec((None, t, MIX_W), lambda b, i: (b, 0, COL_DV)),
                  full(lam_vecs.shape), full(consts.shape), full(norm_g.shape)],
        out_specs=pl.BlockSpec((None, tq, MIX_W), lambda b, i: (b, i, 0)),
        out_shape=jax.ShapeDtypeStruct((bsz, t, MIX_W), F32),
        scratch_shapes=[pltpu.VMEM((t, MIX_W), BF16), pltpu.VMEM((N_HEADS, t, LANES), BF16),
                        pltpu.VMEM((N_HEADS, 2 * tq, LANES), BF16),
                        pltpu.VMEM((N_HEADS, 2 * tq, LANES), F32), pltpu.VMEM((N_HEADS, 2 * tq, LANES), F32)],
        compiler_params=_cparams("parallel", "arbitrary"),
    )(u3, u3, u3, lam_vecs, consts, norm_g)


def _neg_log2_keep(z2):
    e = jnp.exp2(jnp.minimum(z2, -z2))
    return jnp.maximum(z2, 0.0) + jnp.log2(1.0 + e)


def _strict_upper(n):
    return jnp.where(_iota((n, n), 0) > _iota((n, n), 1), 1.0, 0.0).astype(BF16)


def _sb_prompt_kernel(q_ref, k_ref, v_ref, y_ref, kb_s, vb_s, qs_s, r_s, acc_s, hl_s, tot_s, w_s, *, tk):
    i = pl.program_id(1)
    tq = q_ref.shape[0]
    t_all = k_ref.shape[0]
    nl = tk // LANES
    per_tile = tq // tk

    @pl.when(i == 0)
    def _():
        def prep(c, carry):
            r0 = pl.multiple_of(c * tq, tq)
            kb_s[pl.ds(r0, tq), :] = k_ref[pl.ds(r0, tq), :].astype(BF16)
            vb_s[pl.ds(r0, tq), :] = v_ref[pl.ds(r0, tq), :].astype(BF16)
            return carry

        lax.fori_loop(0, t_all // tq, prep, 0)

    q = q_ref[...]
    lane = _iota((1, LANES), 1)
    for h in range(N_HEADS):
        slab, lo = _head_slab(h)
        qs_s[h] = jnp.where((lane >= lo) & (lane < lo + HEAD_DIM), q[:, slab] * (HEAD_DIM ** -0.5 * LOG2E),
                            0.0).astype(BF16)
    r_s[...] = jnp.zeros_like(r_s)
    acc_s[...] = jnp.zeros_like(acc_s)
    col = _iota((ROW_CHUNK, LANES), 1)
    row0 = _iota((ROW_CHUNK, LANES), 0)
    tri = jnp.where(_iota((2 * tk, tk), 0) % tk >= _iota((2 * tk, tk), 1), 1.0, 0.0).astype(BF16)
    chunks = [slice(r, r + ROW_CHUNK) for r in range(0, tq, ROW_CHUNK)]

    def block(j, key_off):
        k0 = pl.multiple_of(j * tk, tk)
        r0 = 0 if key_off is None else key_off
        rows = slice(r0, tq)
        blk_chunks = [rs for rs in chunks if rs.start >= r0]
        local = lambda rs: slice(rs.start - r0, rs.stop - r0)

        def earlier(rs, c):
            return col + (c * LANES + key_off) < row0 + rs.start

        z2s = [_dot_nt(qs_s[h, rows, :], kb_s[pl.ds(k0, tk), _head_slab(h)[0]])
               for h in range(N_HEADS)]
        for h in range(N_HEADS):
            for rs in blk_chunks:
                nk = _neg_log2_keep(z2s[h][local(rs), :])
                if key_off is not None:
                    nk = jnp.concatenate([jnp.where(earlier(rs, c), nk[:, c * LANES:(c + 1) * LANES], 0.0)
                                          for c in range(nl)], axis=1)
                hi = nk.astype(BF16)
                hl_s[h, rs, 0:tk] = hi
                hl_s[h, rs, tk:2 * tk] = (nk - hi.astype(F32)).astype(BF16)
                tot_s[h, rs, :] = jnp.broadcast_to(jnp.sum(nk, axis=1, keepdims=True), (ROW_CHUNK, LANES))
        incl = [_dot(hl_s[h, rows, :], tri) for h in range(N_HEADS)]
        for h in range(N_HEADS):
            for rs in blk_chunks:
                rr = r_s[h, rs, :]
                z2 = z2s[h][local(rs), :]
                inc = incl[h][local(rs), :]
                ws = []
                for c in range(nl):
                    sl = slice(c * LANES, (c + 1) * LANES)
                    wc = jnp.exp2(jnp.minimum(z2[:, sl] - inc[:, sl] - rr, 0.0))
                    if key_off is not None:
                        wc = jnp.where(earlier(rs, c), wc, 0.0)
                    ws.append(wc)
                w_s[h, rs, :] = jnp.concatenate(ws, axis=1).astype(BF16)
                r_s[h, rs, :] = rr + tot_s[h, rs, :]
            acc_s[h, rows, :] = acc_s[h, rows, :] + _dot(w_s[h, rows, :], vb_s[pl.ds(k0, tk), _head_slab(h)[0]])

    for d in range(per_tile - 1, -1, -1):
        block(i * per_tile + d, d * tk)

    def body(jj, carry):
        block(i * per_tile - 1 - jj, None)
        return carry

    lax.fori_loop(0, i * per_tile, body, 0)
    y_ref[...] = jnp.concatenate([acc_s[h][:, _head_slab(h)[1]:_head_slab(h)[1] + HEAD_DIM]
                                  for h in range(N_HEADS)], axis=1)


def _sb_prompt(u3, tq=512, tk=256):
    bsz, t, _ = u3.shape
    return pl.pallas_call(
        functools.partial(_sb_prompt_kernel, tk=tk),
        grid=(bsz, t // tq),
        in_specs=[pl.BlockSpec((None, tq, MIX_W), lambda b, i: (b, i, COL_SQ)),
                  pl.BlockSpec((None, t, MIX_W), lambda b, i: (b, 0, COL_SK)),
                  pl.BlockSpec((None, t, MIX_W), lambda b, i: (b, 0, COL_SV))],
        out_specs=pl.BlockSpec((None, tq, MIX_W), lambda b, i: (b, i, 0)),
        out_shape=jax.ShapeDtypeStruct((bsz, t, MIX_W), F32),
        scratch_shapes=[pltpu.VMEM((t, MIX_W), BF16), pltpu.VMEM((t, MIX_W), BF16),
                        pltpu.VMEM((N_HEADS, tq, LANES), BF16),
                        pltpu.VMEM((N_HEADS, tq, LANES), F32), pltpu.VMEM((N_HEADS, tq, LANES), F32),
                        pltpu.VMEM((N_HEADS, tq, 2 * tk), BF16), pltpu.VMEM((N_HEADS, tq, LANES), F32),
                        pltpu.VMEM((N_HEADS, tq, tk), BF16)],
        compiler_params=_cparams("parallel", "arbitrary"),
    )(u3, u3, u3)


def _decode_attn_one(dk_refs, dv_refs, sk_refs, sv_refs, dq, dkn, dvn, sq, lam_ref, cst_ref, g_ref):
    past = len(dk_refs) * PAGE_SIZE
    w = MIX_W
    lane = _iota((1, w), 1)
    row8 = _iota((2 * N_HEADS, 1), 0)
    cat = lambda page_refs: jnp.concatenate([r[...].astype(BF16) for r in page_refs], axis=1)

    q = dq * (DIFF_QK_DIM ** -0.5 * LOG2E)
    seg = lane // DIFF_QK_DIM
    qrows = jnp.where(seg == row8, q, 0.0)
    slope2 = jnp.exp((row8 // 2 + 1).astype(F32) * (-8.0 * math.log(2.0) / N_HEADS)) * LOG2E
    kpos = _iota((1, past), 1).astype(F32)
    sc = _dot(qrows.astype(BF16), cat(dk_refs)) - slope2 * (float(past) - kpos)
    s_new = jnp.sum(qrows * dkn, axis=1, keepdims=True)
    m = jnp.maximum(s_new, jnp.max(sc, axis=1, keepdims=True))
    p_new = jnp.exp2(s_new - m)
    pr = jnp.exp2(sc - m)
    l = p_new + jnp.sum(pr, axis=1, keepdims=True)
    acc = p_new * dvn + _dot_nt(pr.astype(BF16), cat(dv_refs))
    o = acc / l
    lam, lam_init = _diff_lambda(lam_ref, cst_ref)
    coef = jnp.where(row8 % 2 == 0, 1.0, -lam)
    head_of_lane = lane // HEAD_DIM
    o = jnp.where(head_of_lane == row8 // 2, o * coef, 0.0)
    o = jnp.sum(o, axis=0, keepdims=True)
    ms = _split_dot(o * o, _head_ones(w).astype(BF16)) * (1.0 / HEAD_DIM)
    yb = o * lax.rsqrt(ms + RMS_EPS) * g_ref[...] * (1.0 - lam_init)

    rowh = _iota((N_HEADS, 1), 0)
    qsb = jnp.where(head_of_lane == rowh, sq * (HEAD_DIM ** -0.5 * LOG2E), 0.0).astype(BF16)
    z2 = _dot(qsb, cat(sk_refs))
    nk = _neg_log2_keep(z2)
    chunk = 2 * PAGE_SIZE
    upper = _strict_upper(chunk)
    run = jnp.zeros((N_HEADS, 1), F32)
    later = [None] * (past // chunk)
    for c in range(past // chunk - 1, -1, -1):
        nk_c = nk[:, c * chunk:(c + 1) * chunk]
        later[c] = _split_dot(nk_c, upper) + run
        run = run + jnp.sum(nk_c, axis=1, keepdims=True)
    wgt = jnp.exp2(jnp.minimum(z2 - nk - jnp.concatenate(later, axis=1), 0.0))
    acc = _dot_nt(wgt.astype(BF16), cat(sv_refs))
    yc = jnp.sum(jnp.where(head_of_lane == rowh, acc, 0.0), axis=0, keepdims=True)
    return yb, yc


N_CACHES = 4


def _decode_attn_kernel(pt_ref, dkc_ref, dvc_ref, skc_ref, svc_ref, dq_ref, dkn_ref, dvn_ref, sq_ref,
                        lam_ref, cst_ref, g_ref, yb_ref, yc_ref, page_buf, page_sem, *, layer, n_pages, n_seq):
    b = pl.program_id(0)
    slot = lax.rem(b, 2)
    caches = (dkc_ref, dvc_ref, skc_ref, svc_ref)

    def page_copy(step, slot_, c, si, p):
        page = pt_ref[step * n_seq + si, p]
        return pltpu.make_async_copy(caches[c].at[layer, page], page_buf.at[slot_, c, si * n_pages + p],
                                     page_sem.at[slot_])

    def all_pages(step, slot_):
        return [page_copy(step, slot_, c, si, p)
                for c in range(N_CACHES) for si in range(n_seq) for p in range(n_pages)]

    @pl.when(b == 0)
    def _():
        for cp in all_pages(0, 0):
            cp.start()

    @pl.when(b + 1 < pl.num_programs(0))
    def _():
        for cp in all_pages(b + 1, 1 - slot):
            cp.start()

    for cp in all_pages(b, slot):
        cp.wait()

    for si in range(n_seq):
        pages = [[page_buf.at[slot, c, si * n_pages + p] for p in range(n_pages)] for c in range(N_CACHES)]
        yb, yc = _decode_attn_one(*pages, dq_ref[si], dkn_ref[si], dvn_ref[si], sq_ref[si],
                                  lam_ref, cst_ref, g_ref)
        yb_ref[si] = yb
        yc_ref[si] = yc


def _decode_attn(layer, page_table, caches, u3, lam_vecs, consts, norm_g_tiled, n_seq=2):
    n, n_pages = page_table.shape

    def col_spec(c):
        return pl.BlockSpec((n_seq, 1, MIX_W), lambda b, pt: (b, 0, c))

    full = lambda shape: pl.BlockSpec(shape, lambda b, pt: (0,) * len(shape))
    in_specs = [pl.BlockSpec(memory_space=pl.ANY)] * N_CACHES
    in_specs += [col_spec(COL_DQ), col_spec(COL_DK), col_spec(COL_DV), col_spec(COL_SQ),
                 full(lam_vecs.shape), full(consts.shape), full(norm_g_tiled.shape)]
    out_spec = pl.BlockSpec((n_seq, 1, MIX_W), lambda b, pt: (b, 0, 0))
    return pl.pallas_call(
        functools.partial(_decode_attn_kernel, layer=layer, n_pages=n_pages, n_seq=n_seq),
        grid_spec=pltpu.PrefetchScalarGridSpec(
            num_scalar_prefetch=1, grid=(n // n_seq,), in_specs=in_specs, out_specs=[out_spec, out_spec],
            scratch_shapes=[pltpu.VMEM((2, N_CACHES, n_seq * n_pages, MIX_W, PAGE_SIZE), F32),
                            pltpu.SemaphoreType.DMA((2,))]),
        out_shape=[jax.ShapeDtypeStruct((n, 1, MIX_W), F32), jax.ShapeDtypeStruct((n, 1, MIX_W), F32)],
        compiler_params=_cparams("arbitrary"),
    )(page_table, *caches, u3, u3, u3, u3, lam_vecs, consts, norm_g_tiled)


def _hgrn_gates(hq, hf, lb):
    q = _silu(hq)
    e = jnp.exp(-jnp.abs(hf))
    inv = 1.0 / (1.0 + e)
    pos = hf >= 0.0
    sig = jnp.where(pos, inv, e * inv)
    sig_n = jnp.where(pos, e * inv, inv)
    f = lb + (1.0 - lb) * sig
    return q, f, (1.0 - lb) * sig_n


def _hgrn_out(o, hg, g, ones_bf16):
    ms = _split_dot(o * o, ones_bf16) * (1.0 / HEAD_DIM)
    return o * lax.rsqrt(ms + RMS_EPS) * g * _silu(hg)


def _hgrn_prompt_kernel(hq_ref, hf_ref, hi_ref, hg_ref, lb_ref, g_ref, y_ref, st_ref, st_s):
    t = pl.program_id(1)
    tt = hq_ref.shape[0]
    c = HG_CHUNK
    w = MIX_W

    @pl.when(t == 0)
    def _():
        st_s[...] = jnp.zeros_like(st_s)

    lb = lb_ref[...]
    ones_f = _head_ones(w)
    ones_b = ones_f.astype(BF16)
    tril = jnp.where(_iota((c, c), 1) <= _iota((c, c), 0), 1.0, 0.0).astype(BF16)
    row = _iota((c, 1), 0)
    for ci in range(tt // c):
        sl = slice(ci * c, (ci + 1) * c)
        q, f, k = _hgrn_gates(hq_ref[sl, :], hf_ref[sl, :], lb)
        v = hi_ref[sl, :]
        cum = _split_dot_left(tril, jnp.log(jnp.maximum(f, F_FLOOR)))
        st = st_s[...]
        o = _dot_nt((q * jnp.exp(cum)).astype(BF16), st.astype(BF16))
        for j in range(c // HG_SUB - 1):
            s0, s1 = j * HG_SUB, (j + 1) * HG_SUB
            ref_row = cum[s1 - 1:s1, :]
            qj = q * jnp.exp(jnp.minimum(cum - ref_row, 0.0))
            kj = k[s0:s1, :] * jnp.exp(ref_row - cum[s0:s1, :])
            mt = _dot_tn(v[s0:s1, :].astype(BF16), kj.astype(BF16)) * ones_f
            oj = _dot_nt(qj.astype(BF16), mt.astype(BF16))
            o = o + jnp.where(row >= s1, oj, 0.0)
        for lag in range(HG_SUB):
            if lag == 0:
                ks, cs, vs = k, cum, v
            else:
                ks = pltpu.roll(k, lag, 0)
                cs = pltpu.roll(cum, lag, 0)
                vs = pltpu.roll(v, lag, 0)
            term = q * ks * jnp.exp(jnp.minimum(cum - cs, 0.0))
            ssum = _dot(term.astype(BF16), ones_b)
            o = o + jnp.where(row % HG_SUB >= lag, ssum * vs, 0.0)
        last = cum[c - 1:c, :]
        kc = k * jnp.exp(last - cum)
        st_s[...] = st * jnp.exp(last) + _dot_tn(v.astype(BF16), kc.astype(BF16)) * ones_f
        y_ref[sl, :] = _hgrn_out(o, hg_ref[sl, :], g_ref[...], ones_b)
    st_ref[...] = st_s[...]


def _split_dot_left(w_bf16, x):
    hi = x.astype(BF16)
    lo = (x - hi.astype(F32)).astype(BF16)
    return _dot(w_bf16, hi) + _dot(w_bf16, lo)


def _hgrn_prompt(u3, lb, g_tiled, tt=256):
    bsz, t, _ = u3.shape
    full = lambda shape: pl.BlockSpec(shape, lambda b, i: (0,) * len(shape))
    col = lambda c: pl.BlockSpec((None, tt, MIX_W), lambda b, i: (b, i, c))
    return pl.pallas_call(
        _hgrn_prompt_kernel,
        grid=(bsz, t // tt),
        in_specs=[col(COL_HQ), col(COL_HF), col(COL_HI), col(COL_HG), full((1, MIX_W)), full((1, MIX_W))],
        out_specs=[pl.BlockSpec((None, tt, MIX_W), lambda b, i: (b, i, 0)),
                   pl.BlockSpec((None, MIX_W, MIX_W), lambda b, i: (b, 0, 0))],
        out_shape=[jax.ShapeDtypeStruct((bsz, t, MIX_W), F32),
                   jax.ShapeDtypeStruct((bsz, MIX_W, MIX_W), F32)],
        scratch_shapes=[pltpu.VMEM((MIX_W, MIX_W), F32)],
        compiler_params=_cparams("parallel", "arbitrary"),
    )(u3, u3, u3, u3, lb, g_tiled)


def _hgrn_decode_kernel(s0_ref, q_ref, f_ref, v_ref, lb_ref, sn_ref, o_ref):
    bb = s0_ref.shape[0]
    q, f, k = _hgrn_gates(q_ref[...], f_ref[...], lb_ref[...])
    v = jnp.broadcast_to(v_ref[...], (bb, N_HEADS, HEAD_DIM, HEAD_DIM)).reshape(bb, MIX_W, HEAD_DIM)
    sn = f * s0_ref[...] + k * v
    sn_ref[...] = sn
    o_ref[...] = jnp.sum((q * sn).reshape(bb, N_HEADS, HEAD_DIM, HEAD_DIM), axis=2)


def _hgrn_decode(s0, q3, f3, v4, lb3, bb=8):
    n = s0.shape[0]
    blk = pl.BlockSpec((bb, MIX_W, HEAD_DIM), lambda i: (i, 0, 0))
    col = pl.BlockSpec((bb, MIX_W, 1), lambda i: (i, 0, 0))
    return pl.pallas_call(
        _hgrn_decode_kernel,
        grid=(n // bb,),
        in_specs=[blk, col, col, pl.BlockSpec((bb, N_HEADS, 1, HEAD_DIM), lambda i: (i, 0, 0, 0)),
                  pl.BlockSpec((1, MIX_W, 1), lambda i: (0, 0, 0))],
        out_specs=[blk, pl.BlockSpec((bb, N_HEADS, HEAD_DIM), lambda i: (i, 0, 0))],
        out_shape=[jax.ShapeDtypeStruct((n, MIX_W, HEAD_DIM), F32),
                   jax.ShapeDtypeStruct((n, N_HEADS, HEAD_DIM), F32)],
        compiler_params=_cparams("parallel"),
    )(s0, q3, f3, v4, lb3)


def _hgrn_out_kernel(o_ref, hg_ref, g_ref, y_ref):
    y_ref[...] = _hgrn_out(o_ref[...], hg_ref[...], g_ref[...], _head_ones(MIX_W).astype(BF16))


def _hgrn_decode_out(o, u, g_tiled):
    n = o.shape[0]
    full = lambda shape: pl.BlockSpec(shape, lambda i: (0,) * len(shape))
    return pl.pallas_call(
        _hgrn_out_kernel,
        grid=(1,),
        in_specs=[full((n, MIX_W)), pl.BlockSpec((n, MIX_W), lambda i: (0, COL_HG)), full((1, MIX_W))],
        out_specs=full((n, MIX_W)),
        out_shape=jax.ShapeDtypeStruct((n, MIX_W), F32),
        compiler_params=_cparams("arbitrary"),
    )(o, u, g_tiled)


def _merge_kernel(ya_ref, yb_ref, yc_ref, yd_ref, x_ref, wg0_ref, wg1_ref, wg2_ref, wg3_ref, wb_ref, wo_ref,
                  lg_ref, lbias_ref, o_ref, *, alpha):
    x = x_ref[...]
    xb = x.astype(BF16)
    merged = None
    for n, (y_ref, wg_ref) in enumerate(zip((ya_ref, yb_ref, yc_ref, yd_ref), (wg0_ref, wg1_ref, wg2_ref, wg3_ref))):
        gate = _sigmoid(_dot(xb, wg_ref[...]))
        term = gate * _dot(y_ref[...].astype(BF16), wb_ref[n])
        merged = term if merged is None else merged + term
    out = _dot(merged.astype(BF16), wo_ref[...])
    o_ref[...] = _layer_norm(alpha * x + out, lg_ref[...], lbias_ref[...])


def _merge(ya, yb, yc, yd, x, w_in, wb, wo, ln_g, ln_b, alpha, layer):
    n, d = x.shape
    tm = min(512, n)
    gate0 = N_MIX_COLS * MIX_W // d
    row = lambda width: pl.BlockSpec((tm, width), lambda i: (i, 0))
    full = lambda shape: pl.BlockSpec(shape, lambda i: (0,) * len(shape))
    gate = lambda b: pl.BlockSpec((None, d, d), lambda i, b=b: (layer, 0, gate0 + b))
    of_layer = lambda a: pl.BlockSpec((None,) + a.shape[1:], lambda i: (layer,) + (0,) * (a.ndim - 1))
    return pl.pallas_call(
        functools.partial(_merge_kernel, alpha=alpha),
        grid=(n // tm,),
        in_specs=[row(MIX_W), row(MIX_W), row(MIX_W), row(MIX_W), row(d),
                  gate(0), gate(1), gate(2), gate(3),
                  of_layer(wb), of_layer(wo), full((1, d)), full((1, d))],
        out_specs=row(d),
        out_shape=jax.ShapeDtypeStruct((n, d), F32),
        compiler_params=_cparams("parallel"),
    )(ya, yb, yc, yd, x, w_in, w_in, w_in, w_in, wb, wo, ln_g, ln_b)


def _router_weights(x, wr_hi, wr_lo, br):
    xh = x.astype(BF16)
    xl = (x - xh.astype(F32)).astype(BF16)
    logits = _dot(xh, wr_hi) + _dot(xh, wr_lo) + _dot(xl, wr_hi) + br
    lane = _iota((1, ROUTER_LANES), 1)
    big = jnp.int32(ROUTER_LANES)
    is_g = lane < N_GROUPS
    gl = jnp.where(is_g, logits, NEG_BIG)
    gmax = jnp.max(gl, axis=1, keepdims=True)
    g_idx = jnp.min(jnp.where(is_g & (gl == gmax), lane, big), axis=1, keepdims=True)
    g_w = 1.0 / jnp.sum(jnp.where(is_g, jnp.exp(gl - gmax), 0.0), axis=1, keepdims=True)
    in_grp = (lane >= ROUTER_E0) & (lane < ROUTER_E0 + N_EXPERTS) & \
             ((lane - ROUTER_E0) // EXPERTS_PER_GROUP == g_idx)
    el = jnp.where(in_grp, logits, NEG_BIG)
    v1 = jnp.max(el, axis=1, keepdims=True)
    i1 = jnp.min(jnp.where(in_grp & (el == v1), lane, big), axis=1, keepdims=True)
    el2 = jnp.where(lane == i1, NEG_BIG, el)
    v2 = jnp.max(el2, axis=1, keepdims=True)
    i2 = jnp.min(jnp.where(in_grp & (lane != i1) & (el2 == v2), lane, big), axis=1, keepdims=True)
    e2 = jnp.exp(v2 - v1)
    w1 = g_w / (1.0 + e2)
    w2 = g_w * e2 / (1.0 + e2)
    return jnp.where(lane == i1, w1, 0.0) + jnp.where(lane == i2, w2, 0.0)


def _moe_kernel(x_ref, wrh_ref, wrl_ref, br_ref, w1_ref, w3_ref, w2_ref, lg_ref, lbias_ref, o_ref,
                comb_s, acc_s, xb_s, *, alpha):
    g = pl.program_id(1)
    ff = w1_ref.shape[2]
    tm = x_ref.shape[0]

    @pl.when(g == 0)
    def _():
        x = x_ref[...]
        comb_s[...] = _router_weights(x, wrh_ref[...], wrl_ref[...], br_ref[...])
        xb_s[...] = x.astype(BF16)
        acc_s[...] = jnp.zeros_like(acc_s)

    lane = _iota((1, ROUTER_LANES), 1)
    comb = comb_s[...]
    cexp = []
    for e in range(EXPERTS_PER_GROUP):
        c_e = jnp.sum(jnp.where(lane == ROUTER_E0 + g * EXPERTS_PER_GROUP + e, comb, 0.0), axis=1, keepdims=True)
        cexp.append(jnp.broadcast_to(c_e, (tm, ff)))
    xb = xb_s[...]
    experts = range(EXPERTS_PER_GROUP)
    h1 = jnp.concatenate([_dot(xb, w1_ref[e]) for e in experts], axis=1)
    h3 = jnp.concatenate([_dot(xb, w3_ref[e]) for e in experts], axis=1)
    hid = _silu(h1) * h3 * jnp.concatenate(cexp, axis=1)
    acc_s[...] += _dot(hid.astype(BF16), w2_ref[...].reshape(EXPERTS_PER_GROUP * ff, -1))

    @pl.when(g == pl.num_programs(1) - 1)
    def _():
        o_ref[...] = _layer_norm(alpha * x_ref[...] + acc_s[...], lg_ref[...], lbias_ref[...])


def _moe(x, wr_hi, wr_lo, br, w1, w3, w2, ln_g, ln_b, alpha, layer):
    n, d = x.shape
    _, ne, _, ff = w1.shape
    tm = min(1024, n)
    epg = EXPERTS_PER_GROUP
    full = lambda shape: pl.BlockSpec(shape, lambda i, g: (0,) * len(shape))
    return pl.pallas_call(
        functools.partial(_moe_kernel, alpha=alpha),
        grid=(n // tm, ne // epg),
        in_specs=[pl.BlockSpec((tm, d), lambda i, g: (i, 0)),
                  full(wr_hi.shape), full(wr_lo.shape), full(br.shape),
                  pl.BlockSpec((None, epg, d, ff), lambda i, g: (layer, g, 0, 0)),
                  pl.BlockSpec((None, epg, d, ff), lambda i, g: (layer, g, 0, 0)),
                  pl.BlockSpec((None, epg, ff, d), lambda i, g: (layer, g, 0, 0)),
                  full((1, d)), full((1, d))],
        out_specs=pl.BlockSpec((tm, d), lambda i, g: (i, 0)),
        out_shape=jax.ShapeDtypeStruct((n, d), F32),
        scratch_shapes=[pltpu.VMEM((tm, ROUTER_LANES), F32), pltpu.VMEM((tm, d), F32),
                        pltpu.VMEM((tm, d), BF16)],
        compiler_params=_cparams("parallel", "arbitrary"),
    )(x, wr_hi, wr_lo, br, w1, w3, w2, ln_g, ln_b)


def _block_diag(w):
    nb, n, _ = w.shape
    eye = jnp.eye(nb, dtype=w.dtype)
    return (eye[:, None, :, None] * w[:, :, None, :]).reshape(nb * n, nb * n)


def _router_matrix(wg, bg, we, be):
    d = wg.shape[0]
    wr = jnp.zeros((d, ROUTER_LANES), F32)
    wr = wr.at[:, 0:N_GROUPS].set(wg).at[:, ROUTER_E0:ROUTER_E0 + N_EXPERTS].set(we)
    br = jnp.zeros((1, ROUTER_LANES), F32)
    br = br.at[0, 0:N_GROUPS].set(bg).at[0, ROUTER_E0:ROUTER_E0 + N_EXPERTS].set(be)
    hi = wr.astype(BF16)
    lo = (wr - hi.astype(F32)).astype(BF16)
    return hi, lo, br


def kernel(x_prompt, x_sample, cache_diff_k, cache_diff_v, cache_sb_k, cache_sb_v, page_table, state_conv, state_lru, state_hgrn, w_in, conv_w, conv_b, lru_wa, lru_ba, lru_wx, lru_bx, lru_lambda, diff_lam_q1, diff_lam_k1, diff_lam_q2, diff_lam_k2, diff_norm_g, hgrn_lb_raw, hgrn_norm_g, w_branch, w_out, ln1_g, ln1_b, router_group_w, router_group_b, router_expert_w, router_expert_b, exp_w1, exp_w3, exp_w2, ln2_g, ln2_b):
    depth = w_in.shape[0]
    bsz, seq, d = x_prompt.shape
    nd = x_sample.shape[0]
    n_pool = cache_diff_k.shape[1]
    alpha = (2 * depth) ** 0.25
    row = lambda a: a.reshape(1, -1)

    hg_lb = _hgrn_lower_bounds(hgrn_lb_raw)
    caches = [jnp.transpose(c, (0, 1, 3, 4, 2)).reshape(depth, n_pool, MIX_W, PAGE_SIZE)
              for c in (cache_diff_k, cache_diff_v, cache_sb_k, cache_sb_v)]

    xp = x_prompt.reshape(bsz * seq, d)
    xs = x_sample.reshape(nd, d)
    outs_p = [[] for _ in range(3)]
    kv_p = [jnp.zeros((depth, bsz, MIX_W, seq), F32) for _ in KV_COLS]
    outs_s = [[] for _ in range(7)]
    w_in_b = w_in.astype(BF16)
    wb, wo = w_branch.astype(BF16), w_out.astype(BF16)
    w1, w3, w2 = exp_w1.astype(BF16), exp_w3.astype(BF16), exp_w2.astype(BF16)
    for l in range(depth):
        wa = _block_diag(lru_wa[l]).astype(BF16)
        wx = _block_diag(lru_wx[l]).astype(BF16)
        lru = (conv_w[l], row(conv_b[l]), wa, row(lru_ba[l]), wx, row(lru_bx[l]), row(lru_lambda[l]))
        lam_vecs = jnp.stack([diff_lam_q1[l], diff_lam_k1[l], diff_lam_q2[l], diff_lam_k2[l]])
        consts = jnp.zeros((1, 128), F32).at[0, 0].set(0.8 - 0.6 * math.exp(-0.3 * l))
        dn_g = row(diff_norm_g[l])
        dn_g_t = jnp.tile(dn_g, (1, N_HEADS))
        hg_g_t = jnp.tile(row(hgrn_norm_g[l]), (1, N_HEADS))
        lb = row(hg_lb[l])
        wr_hi, wr_lo, br = _router_matrix(router_group_w[l], router_group_b[l], router_expert_w[l], router_expert_b[l])
        merge_w = (w_in_b, wb, wo, row(ln1_g[l]), row(ln1_b[l]), alpha, l)
        moe_w = (wr_hi, wr_lo, br, w1, w3, w2, row(ln2_g[l]), row(ln2_b[l]), alpha, l)

        u, *kv_p = _in_proj_kv(xp, w_in_b, l, kv_p)
        u3 = u.reshape(bsz, seq, -1)
        ya, conv_p, lru_p = _rglru_prompt(u3, *lru)
        yb = _diff_prompt(u3, lam_vecs, consts, dn_g)
        yc = _sb_prompt(u3)
        yd, st_t = _hgrn_prompt(u3, lb, hg_g_t)
        flat = lambda a: a.reshape(bsz * seq, MIX_W)
        x1 = _merge(flat(ya), flat(yb), flat(yc), flat(yd), xp, *merge_w)
        xp = _moe(x1, *moe_w)
        st = jnp.stack([st_t[:, h * HEAD_DIM:(h + 1) * HEAD_DIM, h * HEAD_DIM:(h + 1) * HEAD_DIM]
                        for h in range(N_HEADS)], axis=1).swapaxes(-1, -2)
        for lst, val in zip(outs_p, (conv_p, lru_p.reshape(bsz, MIX_W), st)):
            lst.append(val)

        us = _in_proj(xs, w_in_b, l)
        ya, conv_s, lru_s = _rglru_decode(us, state_conv[l].reshape(nd, -1), state_lru[l], *lru)
        yb, yc = _decode_attn(l, page_table, caches, us.reshape(nd, 1, -1), lam_vecs, consts, dn_g_t)
        colb = lambda c: us[:, c * MIX_W:(c + 1) * MIX_W].reshape(nd, MIX_W, 1)
        v4 = us[:, COL_HI * MIX_W:(COL_HI + 1) * MIX_W].reshape(nd, N_HEADS, 1, HEAD_DIM)
        hgrn_s, o_d = _hgrn_decode(state_hgrn[l].reshape(nd, MIX_W, HEAD_DIM), colb(COL_HQ), colb(COL_HF), v4,
                                   hg_lb[l].reshape(1, MIX_W, 1))
        yd = _hgrn_decode_out(o_d.reshape(nd, MIX_W), us, hg_g_t)
        x1 = _merge(ya, yb.reshape(nd, MIX_W), yc.reshape(nd, MIX_W), yd, xs, *merge_w)
        xs = _moe(x1, *moe_w)
        heads = lambda c: us[:, c * MIX_W:(c + 1) * MIX_W].reshape(nd, 1, N_HEADS, HEAD_DIM)
        for lst, val in zip(outs_s, (heads(COL_DK), heads(COL_DV), heads(COL_SK), heads(COL_SV),
                                     conv_s.reshape(nd, CONV_WIDTH - 1, MIX_W), lru_s,
                                     hgrn_s.reshape(nd, N_HEADS, HEAD_DIM, HEAD_DIM))):
            lst.append(val)

    stack = lambda lsts: [jnp.stack(v, axis=0) for v in lsts]
    kv_p = [a.reshape(depth, bsz, N_HEADS, HEAD_DIM, seq).transpose(0, 1, 4, 2, 3) for a in kv_p]
    return (xp.reshape(bsz, seq, d), xs.reshape(nd, 1, d), *kv_p, *stack(outs_p), *stack(outs_s))
```
